```python
import math
import jax
import jax.numpy as jnp
from jax import lax
import numpy as np

D_MODEL = 1024
BATCH = 4
SEQ = 8192
DEPTH = 4

GRID_W = 64
CTX_LEN = 256
N_MIXERS = 3
EPS = 1e-6
ROPE_BASE = 10000.0
Q_BLOCK = 128
N_MOD = 6

MLA_HEADS = 8
MLA_NOPE = 128
MLA_ROPE = 64
MLA_V = 128
MLA_Q_RANK = 384
MLA_KV_RANK = 256

HG_HEADS = 8
HG_DK = D_MODEL // HG_HEADS
HG_DV = D_MODEL // HG_HEADS
HG_CHUNK = 64

DF_HEADS = 8
DF_DQK = D_MODEL // (2 * DF_HEADS)
DF_DV = 2 * DF_DQK

D_FF = -(-(8 * D_MODEL) // (3 * 256)) * 256

N_MLA = (DEPTH + N_MIXERS - 1) // N_MIXERS
N_HG = (DEPTH + N_MIXERS - 2) // N_MIXERS
N_DF = DEPTH // N_MIXERS

kernel_name = 'hybrid_mla_hgrn2_diffattn_dit'


def rms_norm(x, w):
    xf = x.astype(jnp.float32)
    y = xf * lax.rsqrt(jnp.mean(xf * xf, axis=-1, keepdims=True) + EPS)
    return (y * w.astype(jnp.float32)).astype(x.dtype)


def modulate(x, shift, scale):
    return x * (1 + scale) + shift


def swiglu(a, w1, w3, w2):
    return (jax.nn.silu(a @ w1) * (a @ w3)) @ w2


def split_heads(t, n_heads):
    b, n, _ = t.shape
    return t.reshape(b, n, n_heads, -1).transpose(0, 2, 1, 3)


def merge_heads(t):
    b, h, n, d = t.shape
    return t.transpose(0, 2, 1, 3).reshape(b, n, h * d)


def axial_rope_tables(rows, rot_dim):
    row = jnp.repeat(jnp.arange(rows, dtype=jnp.float32), GRID_W)
    col = jnp.tile(jnp.arange(GRID_W, dtype=jnp.float32), rows)
    axis_dim = rot_dim // 2
    inv_freq = jnp.power(ROPE_BASE, -jnp.arange(0, axis_dim, 2, dtype=jnp.float32) / axis_dim)
    ang_r = row[:, None] * inv_freq
    ang_c = col[:, None] * inv_freq
    ang = jnp.concatenate([ang_r, ang_r, ang_c, ang_c], axis=-1)
    return jnp.cos(ang), jnp.sin(ang)


def apply_axial_rope(t, cos, sin):
    t1, t2, t3, t4 = jnp.split(t, 4, axis=-1)
    rot = jnp.concatenate([-t2, t1, -t4, t3], axis=-1)
    return (t * cos + rot * sin).astype(t.dtype)


def softmax_attend(q, k, v, scale):
    s = jnp.einsum('bhqd,bhkd->bhqk', q, k, preferred_element_type=jnp.float32) * scale
    p = jax.nn.softmax(s, axis=-1)
    return jnp.einsum('bhqk,bhkd->bhqd', p.astype(v.dtype), v)


def sweep_query_blocks(fn, q):
    n, d = q.shape[-2], q.shape[-1]
    nb = n // Q_BLOCK
    qb = jnp.moveaxis(q.reshape(q.shape[:-2] + (nb, Q_BLOCK, d)), -3, 0)
    o = jnp.moveaxis(lax.map(fn, qb), 0, -3)
    return o.reshape(o.shape[:-3] + (n, o.shape[-1]))


def gla_chunk_scan(q, k, v, log_f, s0):
    b, h, n, _ = q.shape
    nc = n // HG_CHUNK

    def chunks(t):
        return jnp.moveaxis(t.reshape(b, h, nc, HG_CHUNK, t.shape[-1]), 2, 0)

    order = jnp.tril(jnp.ones((HG_CHUNK, HG_CHUNK), dtype=bool))[:, :, None]

    def step(state, inp):
        qc, kc, vc, lfc = inp
        cum = jnp.cumsum(lfc, axis=-2)
        rel = jnp.where(order, cum[..., :, None, :] - cum[..., None, :, :], -jnp.inf)
        att = jnp.einsum('bhtd,bhsd,bhtsd->bhts', qc, kc, jnp.exp(rel))
        o = (jnp.einsum('bhts,bhse->bhte', att, vc)
             + jnp.einsum('bhtd,bhde->bhte', qc * jnp.exp(cum), state))
        last = cum[..., -1:, :]
        state = (jnp.exp(last[..., 0, :])[..., None] * state
                 + jnp.einsum('bhsd,bhse->bhde', kc * jnp.exp(last - cum), vc))
        return state, o

    state, o = lax.scan(step, s0, (chunks(q), chunks(k), chunks(v), chunks(log_f)))
    o = jnp.moveaxis(o, 0, 2).reshape(b, h, n, v.shape[-1])
    return o, state


def mla_mixer(a_lat, a_ctx, rope, w_in, q_norm_w, kv_norm_w, w_uq, w_ukv, qn_w, qr_w, kn_w, kr_w, w_o, update_ctx):
    cos, sin = rope

    def project(a):
        b, n, _ = a.shape
        cq, ckv, kr = jnp.split(a @ w_in, [MLA_Q_RANK, MLA_Q_RANK + MLA_KV_RANK], axis=-1)
        q = split_heads(rms_norm(cq, q_norm_w) @ w_uq, MLA_HEADS)
        kv = split_heads(rms_norm(ckv, kv_norm_w) @ w_ukv, MLA_HEADS)
        q_nope = rms_norm(q[..., :MLA_NOPE], qn_w)
        q_rope = rms_norm(q[..., MLA_NOPE:], qr_w)
        k_nope = rms_norm(kv[..., :MLA_NOPE], kn_w)
        v = kv[..., MLA_NOPE:]
        k_rope = rms_norm(kr, kr_w)[:, None]
        return q_nope, q_rope, k_nope, k_rope, v

    def full_k(kn, kr):
        return jnp.concatenate([kn, jnp.broadcast_to(kr, kn.shape[:-1] + (MLA_ROPE,))], axis=-1)

    qn_l, qr_l, kn_l, kr_l, v_l = project(a_lat)
    qr_l = apply_axial_rope(qr_l, cos, sin)
    kr_l = apply_axial_rope(kr_l, cos, sin)
    qn_c, qr_c, kn_c, kr_c, v_c = project(a_ctx)
    k_c = full_k(kn_c, kr_c)
    k_all = jnp.concatenate([k_c, full_k(kn_l, kr_l)], axis=-2)
    v_all = jnp.concatenate([v_c, v_l], axis=-2)
    scale = 1.0 / math.sqrt(MLA_NOPE + MLA_ROPE)
    q_l = jnp.concatenate([qn_l, qr_l], axis=-1)
    o_l = sweep_query_blocks(lambda qb: softmax_attend(qb, k_all, v_all, scale), q_l)
    out_l = merge_heads(o_l) @ w_o
    out_c = None
    if update_ctx:
        q_c = jnp.concatenate([qn_c, qr_c], axis=-1)
        out_c = merge_heads(softmax_attend(q_c, k_c, v_c, scale)) @ w_o
    return out_l, out_c


def hgrn2_mixer(a_lat, a_ctx, layer_idx, w_in, lb_logits, o_norm_w, w_o, update_ctx):
    lb_cum = jnp.cumsum(jax.nn.softmax(lb_logits.astype(jnp.float32), axis=0), axis=0)
    lb = lb_cum[layer_idx] - lb_cum[0]

    def project(a):
        q, zf, zb, i, g = jnp.split(a @ w_in, 5, axis=-1)

        def heads(t):
            return split_heads(t, HG_HEADS).astype(jnp.float32)

        def log_forget(z, lbd):
            lbd = lbd.reshape(HG_HEADS, 1, HG_DK)
            return jnp.logaddexp(jnp.log(lbd), jnp.log1p(-lbd) + jax.nn.log_sigmoid(heads(z)))

        return heads(q), heads(i), log_forget(zf, lb[0]), log_forget(zb, lb[1]), g

    def scan_dir(q, i, log_f, s0):
        return gla_chunk_scan(q, -jnp.expm1(log_f), i, log_f, s0)

    def rev(t):
        return jnp.flip(t, axis=-2)

    def readout(o, g):
        o = rms_norm(o.astype(a_lat.dtype), o_norm_w)
        return (merge_heads(o) * jax.nn.silu(g)) @ w_o

    q_l, i_l, lf_l, lbk_l, g_l = project(a_lat)
    q_c, i_c, lf_c, lbk_c, g_c = project(a_ctx)
    s0 = jnp.zeros((a_ctx.shape[0], HG_HEADS, HG_DK, HG_DV), jnp.float32)
    o_cf, s_cf = scan_dir(q_c, i_c, lf_c, s0)
    o_cb, s_cb = scan_dir(rev(q_c), rev(i_c), rev(lbk_c), s0)
    o_lf, _ = scan_dir(q_l, i_l, lf_l, s_cf)
    o_lb, _ = scan_dir(rev(q_l), rev(i_l), rev(lbk_l), s_cb)
    out_l = readout(o_lf + rev(o_lb), g_l)
    out_c = readout(o_cf + rev(o_cb), g_c) if update_ctx else None
    return out_l, out_c


def diff_mixer(a_lat, a_ctx, rope, layer_idx, w_qkv, qn_w, kn_w, lam_vec, sub_norm_w, w_o, update_ctx):
    cos, sin = rope
    lam_init = 0.8 - 0.6 * math.exp(-0.3 * layer_idx)
    lv = lam_vec.astype(jnp.float32)
    lam = jnp.exp(jnp.sum(lv[0] * lv[1])) - jnp.exp(jnp.sum(lv[2] * lv[3])) + lam_init

    def project(a):
        b, n, _ = a.shape
        q, k, v = jnp.split(a @ w_qkv, 3, axis=-1)
        q = rms_norm(q.reshape(b, n, DF_HEADS, 2, DF_DQK).transpose(0, 2, 3, 1, 4), qn_w)
        k = rms_norm(k.reshape(b, n, DF_HEADS, 2, DF_DQK).transpose(0, 2, 3, 1, 4), kn_w)
        return q, k, split_heads(v, DF_HEADS)

    scale = 1.0 / math.sqrt(DF_DQK)

    def diff_attend(q, k, v):
        s = jnp.einsum('bhmqd,bhmkd->bhmqk', q, k, preferred_element_type=jnp.float32) * scale
        p = jax.nn.softmax(s, axis=-1)
        a = p[:, :, 0] - lam * p[:, :, 1]
        return jnp.einsum('bhqk,bhkd->bhqd', a.astype(v.dtype), v)

    def readout(o):
        return merge_heads(rms_norm(o, sub_norm_w) * (1.0 - lam_init)) @ w_o

    q_l, k_l, v_l = project(a_lat)
    q_l = apply_axial_rope(q_l, cos, sin)
    k_l = apply_axial_rope(k_l, cos, sin)
    q_c, k_c, v_c = project(a_ctx)
    k_all = jnp.concatenate([k_c, k_l], axis=-2)
    v_all = jnp.concatenate([v_c, v_l], axis=-2)
    out_l = readout(sweep_query_blocks(lambda qb: diff_attend(qb, k_all, v_all), q_l))
    out_c = readout(diff_attend(q_c, k_c, v_c)) if update_ctx else None
    return out_l, out_c


def setup_inputs(seed: int = 0) -> dict:
    key = jax.random.key(seed)
    ks = iter(jax.random.split(key, 40))
    D = D_MODEL

    def nrm(shape, scale):
        return scale * jax.random.normal(next(ks), shape, jnp.float32)

    def gain(shape):
        return 1.0 + nrm(shape, 0.02)

    return {
        'x': nrm((BATCH, SEQ, D), 1.0),
        'c': nrm((BATCH, D), 1.0),
        'ctx': nrm((BATCH, CTX_LEN, D), 1.0),
        'c_ctx': nrm((D,), 1.0),
        'ada_w': nrm((DEPTH, D, N_MOD * D), 0.5 * D ** -0.5),
        'ada_b': nrm((DEPTH, N_MOD * D), 0.02),
        'norm1_w': gain((DEPTH, D)),
        'norm2_w': gain((DEPTH, D)),
        'ffn_w1': nrm((DEPTH, D, D_FF), D ** -0.5),
        'ffn_w3': nrm((DEPTH, D, D_FF), D ** -0.5),
        'ffn_w2': nrm((DEPTH, D_FF, D), D_FF ** -0.5),
        'mla_w_in': nrm((N_MLA, D, MLA_Q_RANK + MLA_KV_RANK + MLA_ROPE), D ** -0.5),
        'mla_q_norm_w': gain((N_MLA, MLA_Q_RANK)),
        'mla_kv_norm_w': gain((N_MLA, MLA_KV_RANK)),
        'mla_w_uq': nrm((N_MLA, MLA_Q_RANK, MLA_HEADS * (MLA_NOPE + MLA_ROPE)), MLA_Q_RANK ** -0.5),
        'mla_w_ukv': nrm((N_MLA, MLA_KV_RANK, MLA_HEADS * (MLA_NOPE + MLA_V)), MLA_KV_RANK ** -0.5),
        'mla_qn_w': gain((N_MLA, MLA_NOPE)),
        'mla_qr_w': gain((N_MLA, MLA_ROPE)),
        'mla_kn_w': gain((N_MLA, MLA_NOPE)),
        'mla_kr_w': gain((N_MLA, MLA_ROPE)),
        'mla_w_o': nrm((N_MLA, MLA_HEADS * MLA_V, D), (MLA_HEADS * MLA_V) ** -0.5),
        'hg_w_in': nrm((N_HG, D, 5 * D), D ** -0.5),
        'hg_lb_logits': nrm((DEPTH, 2, D), 1.0),
        'hg_o_norm_w': gain((N_HG, HG_DV)),
        'hg_w_o': nrm((N_HG, D, D), D ** -0.5),
        'df_w_qkv': nrm((N_DF, D, 3 * D), D ** -0.5),
        'df_qn_w': gain((N_DF, DF_DQK)),
        'df_kn_w': gain((N_DF, DF_DQK)),
        'df_lambda': nrm((N_DF, 4, DF_DQK), 0.1),
        'df_sub_norm_w': gain((N_DF, DF_DV)),
        'df_w_o': nrm((N_DF, D, D), D ** -0.5),
    }


def reference(x, c, ctx, c_ctx, ada_w, ada_b, norm1_w, norm2_w, ffn_w1, ffn_w3, ffn_w2,
              mla_w_in, mla_q_norm_w, mla_kv_norm_w, mla_w_uq, mla_w_ukv, mla_qn_w, mla_qr_w, mla_kn_w,
              mla_kr_w, mla_w_o, hg_w_in, hg_lb_logits, hg_o_norm_w, hg_w_o,
              df_w_qkv, df_qn_w, df_kn_w, df_lambda, df_sub_norm_w, df_w_o):
    rows = x.shape[1] // GRID_W
    rope_mla = axial_rope_tables(rows, MLA_ROPE)
    rope_df = axial_rope_tables(rows, DF_DQK)
    h_ctx = ctx
    for i in range(DEPTH):
        kind, j = i % N_MIXERS, i // N_MIXERS
        update_ctx = i < DEPTH - 1
        mod_l = jnp.split((jax.nn.silu(c) @ ada_w[i] + ada_b[i])[:, None, :], N_MOD, axis=-1)
        mod_c = jnp.split((jax.nn.silu(c_ctx) @ ada_w[i] + ada_b[i])[None, None, :], N_MOD, axis=-1)
        a_l = modulate(rms_norm(x, norm1_w[i]), mod_l[0], mod_l[1])
        a_c = modulate(rms_norm(h_ctx, norm1_w[i]), mod_c[0], mod_c[1])
        if kind == 0:
            o_l, o_c = mla_mixer(a_l, a_c, rope_mla, mla_w_in[j], mla_q_norm_w[j], mla_kv_norm_w[j],
                                 mla_w_uq[j], mla_w_ukv[j], mla_qn_w[j], mla_qr_w[j], mla_kn_w[j],
                                 mla_kr_w[j], mla_w_o[j], update_ctx)
        elif kind == 1:
            o_l, o_c = hgrn2_mixer(a_l, a_c, i, hg_w_in[j], hg_lb_logits, hg_o_norm_w[j], hg_w_o[j], update_ctx)
        else:
            o_l, o_c = diff_mixer(a_l, a_c, rope_df, i, df_w_qkv[j], df_qn_w[j], df_kn_w[j], df_lambda[j],
                                  df_sub_norm_w[j], df_w_o[j], update_ctx)
        x = x + mod_l[2] * o_l
        x = x + mod_l[5] * swiglu(modulate(rms_norm(x, norm2_w[i]), mod_l[3], mod_l[4]),
                                  ffn_w1[i], ffn_w3[i], ffn_w2[i])
        if update_ctx:
            h_ctx = h_ctx + mod_c[2] * o_c
            h_ctx = h_ctx + mod_c[5] * swiglu(modulate(rms_norm(h_ctx, norm2_w[i]), mod_c[3], mod_c[4]),
                                              ffn_w1[i], ffn_w3[i], ffn_w2[i])
    return x
```

```python
import functools
import math

import jax
import jax.numpy as jnp
from jax import lax
from jax.experimental import pallas as pl
from jax.experimental.pallas import tpu as pltpu

F32 = jnp.float32
BF16 = jnp.bfloat16

EPS = 1e-6
ROPE_BASE = 10000.0
GRID_W = 64
N_MOD = 6
N_MIXERS = 3

N_HEADS = 8
LANES = 128
MLA_NOPE = 128
MLA_ROPE = 64
MLA_V = 128
MLA_Q_RANK = 384
MLA_KV_RANK = 256
MLA_DK = 256
ROPE_DIM = 64
ROPE_QUARTER = ROPE_DIM // 4

TOKEN_TILE = 256
ATTN_TQ = 256
ATTN_TK = 256
HG_CHUNK = 64
HG_SUB = 16
VMEM_LIMIT_BYTES = 56 * 1024 * 1024


def _cparams(*sem):
    return pltpu.CompilerParams(dimension_semantics=sem, vmem_limit_bytes=VMEM_LIMIT_BYTES)


def _const_spec(shape):
    nd = len(shape)
    return pl.BlockSpec(shape, lambda *_: (0,) * nd, pipeline_mode=pl.Buffered(1))


def _dot(a, b):
    return jnp.dot(a, b, preferred_element_type=F32)


def _dot_nt(a, b):
    return lax.dot_general(a, b, (((1,), (1,)), ((), ())), preferred_element_type=F32)


def _dot_tn(a, b):
    return lax.dot_general(a, b, (((0,), (0,)), ((), ())), preferred_element_type=F32)


def _silu(t):
    return t * jax.nn.sigmoid(t)


def _rms(t, w):
    return t * lax.rsqrt(jnp.mean(t * t, axis=-1, keepdims=True) + EPS) * w


def _norm_mod(t, w, shift, scale):
    return _rms(t, w) * (1.0 + scale) + shift


def _group_rms(t, w, group):
    width = t.shape[-1]
    lane = lax.broadcasted_iota(jnp.int32, (1, LANES), 1)
    low = lane < 64
    outs = []
    for p in range(width // LANES):
        ts = t[:, p * LANES:(p + 1) * LANES]
        sq = ts * ts
        if group == LANES:
            ms = jnp.sum(sq, axis=-1, keepdims=True) * (1.0 / LANES)
        else:
            s_lo = jnp.sum(jnp.where(low, sq, 0.0), axis=-1, keepdims=True)
            s_hi = jnp.sum(jnp.where(low, 0.0, sq), axis=-1, keepdims=True)
            ms = jnp.where(low, s_lo, s_hi) * (1.0 / 64)
        outs.append(ts * lax.rsqrt(ms + EPS))
    return jnp.concatenate(outs, axis=-1) * w


def _rope(t, cos, sin):
    width = t.shape[-1]
    lane = lax.broadcasted_iota(jnp.int32, (1, width), 1)
    up = pltpu.roll(t, width - ROPE_QUARTER, 1)
    dn = pltpu.roll(t, ROPE_QUARTER, 1)
    rot = jnp.where((lane % (2 * ROPE_QUARTER)) < ROPE_QUARTER, -up, dn)
    return t * cos + rot * sin


def _ada_kernel(c_ref, w_ref, b_ref, o_ref):
    s = _silu(c_ref[...]).astype(BF16)
    o_ref[0] = _dot(s, w_ref[0].astype(BF16)) + b_ref[0]


def _ada_call(cvec, ada_w, ada_b):
    depth, d, n = ada_w.shape
    tn = 768
    return pl.pallas_call(
        _ada_kernel,
        grid=(depth, n // tn),
        in_specs=[
            pl.BlockSpec((8, d), lambda l, j: (0, 0)),
            pl.BlockSpec((1, d, tn), lambda l, j: (l, 0, j)),
            pl.BlockSpec((1, 1, tn), lambda l, j: (l, 0, j)),
        ],
        out_specs=pl.BlockSpec((1, 8, tn), lambda l, j: (l, 0, j)),
        out_shape=jax.ShapeDtypeStruct((depth, 8, n), F32),
        compiler_params=_cparams("parallel", "parallel"),
        name="ada_mod",
    )(cvec, ada_w, ada_b.reshape(depth, 1, n))


def _tok_spec(width, tm=TOKEN_TILE):
    return pl.BlockSpec((1, tm, width), lambda b, i: (b, i, 0))


def _mod_spec(layer, n_lat_tiles, ctx_row, d):
    return pl.BlockSpec((1, 1, N_MOD, d),
                        lambda b, i: (layer, jnp.where(i < n_lat_tiles, b, ctx_row), 0, 0))


def _head_spec(width, tm=TOKEN_TILE):
    return pl.BlockSpec((1, N_HEADS, tm, width), lambda b, i: (b, 0, i, 0))


def _mla_pre_kernel(x_ref, mod_ref, n1_ref, win_ref, qnw_ref, kvnw_ref, wuqn_ref, wuqr_ref,
                    wk_ref, wv_ref, qn_ref, qr_ref, kn_ref, kr_ref, cos_ref, sin_ref,
                    q_ref, k_ref, v_ref, *, scale):
    m = mod_ref[0, 0]
    a = _norm_mod(x_ref[0], n1_ref[...], m[0:1], m[1:2]).astype(BF16)
    t = _dot(a, win_ref[...])
    cq = _rms(t[:, :MLA_Q_RANK], qnw_ref[...]).astype(BF16)
    ckv = _rms(t[:, MLA_Q_RANK:MLA_Q_RANK + MLA_KV_RANK], kvnw_ref[...]).astype(BF16)
    kr = t[:, MLA_Q_RANK + MLA_KV_RANK:]
    cos = cos_ref[...]
    sin = sin_ref[...]
    kr = kr * lax.rsqrt(jnp.sum(kr * kr, axis=-1, keepdims=True) * (1.0 / MLA_ROPE) + EPS) * kr_ref[...]
    kr = _rope(kr, cos[:, :LANES], sin[:, :LANES]).astype(BF16)
    qn = _group_rms(_dot(cq, wuqn_ref[...]), qn_ref[...], LANES) * scale
    qr = _group_rms(_dot(cq, wuqr_ref[...]), qr_ref[...], 64)
    qr = _rope(qr, cos, sin) * scale
    kn = _group_rms(_dot(ckv, wk_ref[...]), kn_ref[...], LANES)
    v = _dot(ckv, wv_ref[...])
    for h in range(N_HEADS):
        sl = slice(h * LANES, (h + 1) * LANES)
        q_ref[0, h, :, :LANES] = qn[:, sl].astype(BF16)
        q_ref[0, h, :, LANES:] = qr[:, sl].astype(BF16)
        k_ref[0, h, :, :LANES] = kn[:, sl].astype(BF16)
        k_ref[0, h, :, LANES:] = kr
        v_ref[0, h] = v[:, sl].astype(BF16)


def _mla_pre_call(h, mods, layer, n_lat_tiles, n1, p, cos_t, sin_t):
    b, nt, d = h.shape
    tm = TOKEN_TILE
    width = N_HEADS * LANES
    weights = [n1, p["w_in"], p["q_norm_w"], p["kv_norm_w"], p["w_uq_n"], p["w_uq_r"], p["w_uk"],
               p["w_uv"], p["qn_w"], p["qr_w"], p["kn_w"], p["kr_w"]]
    scale = 1.0 / math.sqrt(MLA_NOPE + MLA_ROPE)
    return pl.pallas_call(
        functools.partial(_mla_pre_kernel, scale=scale),
        grid=(b, nt // tm),
        in_specs=[_tok_spec(d), _mod_spec(layer, n_lat_tiles, b, d)]
        + [_const_spec(w.shape) for w in weights]
        + [pl.BlockSpec((tm, width), lambda bb, i: (i, 0))] * 2,
        out_specs=[_head_spec(MLA_DK), _head_spec(MLA_DK), _head_spec(MLA_V)],
        out_shape=[jax.ShapeDtypeStruct((b, N_HEADS, nt, MLA_DK), BF16),
                   jax.ShapeDtypeStruct((b, N_HEADS, nt, MLA_DK), BF16),
                   jax.ShapeDtypeStruct((b, N_HEADS, nt, MLA_V), BF16)],
        compiler_params=_cparams("parallel", "parallel"),
        name="mla_pre",
    )(h, mods, *weights, cos_t, sin_t)


def _df_pre_kernel(x_ref, mod_ref, n1_ref, wqkv_ref, qn_ref, kn_ref, cos_ref, sin_ref,
                   q_ref, k_ref, v_ref, *, scale):
    m = mod_ref[0, 0]
    a = _norm_mod(x_ref[0], n1_ref[...], m[0:1], m[1:2]).astype(BF16)
    width = N_HEADS * LANES
    t = _dot(a, wqkv_ref[...])
    cos = cos_ref[...]
    sin = sin_ref[...]
    q = _rope(_group_rms(t[:, :width], qn_ref[...], 64), cos, sin) * scale
    k = _rope(_group_rms(t[:, width:2 * width], kn_ref[...], 64), cos, sin)
    v = t[:, 2 * width:]
    lane = lax.broadcasted_iota(jnp.int32, (1, LANES), 1)
    low = lane < 64
    for h in range(N_HEADS):
        sl = slice(h * LANES, (h + 1) * LANES)
        qh = q[:, sl]
        q_ref[0, h, 0] = jnp.where(low, qh, 0.0).astype(BF16)
        q_ref[0, h, 1] = jnp.where(low, 0.0, qh).astype(BF16)
        k_ref[0, h] = k[:, sl].astype(BF16)
        v_ref[0, h] = v[:, sl].astype(BF16)


def _df_pre_call(h, mods, layer, n_lat_tiles, n1, p, cos_t, sin_t):
    b, nt, d = h.shape
    tm = TOKEN_TILE
    width = N_HEADS * LANES
    weights = [n1, p["w_qkv"], p["qn_w"], p["kn_w"]]
    return pl.pallas_call(
        functools.partial(_df_pre_kernel, scale=1.0 / math.sqrt(ROPE_DIM)),
        grid=(b, nt // tm),
        in_specs=[_tok_spec(d), _mod_spec(layer, n_lat_tiles, b, d)]
        + [_const_spec(w.shape) for w in weights]
        + [pl.BlockSpec((tm, width), lambda bb, i: (i, 0))] * 2,
        out_specs=[pl.BlockSpec((1, N_HEADS, 2, tm, LANES), lambda bb, i: (bb, 0, 0, i, 0)),
                   _head_spec(LANES), _head_spec(LANES)],
        out_shape=[jax.ShapeDtypeStruct((b, N_HEADS, 2, nt, LANES), BF16),
                   jax.ShapeDtypeStruct((b, N_HEADS, nt, LANES), BF16),
                   jax.ShapeDtypeStruct((b, N_HEADS, nt, LANES), BF16)],
        compiler_params=_cparams("parallel", "parallel"),
        name="df_pre",
    )(h, mods, *weights, cos_t, sin_t)


def _flash_kernel(*refs, n_sub, tk, n_lat_q, n_lat_kv, n_all_kv, lam_init):
    if n_sub == 2:
        q_ref, k_ref, v_ref, lam_ref, snw_ref, o_ref = refs
    else:
        q_ref, k_ref, v_ref, o_ref = refs
    qi = pl.program_id(2)
    first = jnp.where(qi >= n_lat_q, n_lat_kv, 0)
    qs = [q_ref[0, 0, s] for s in range(n_sub)]
    tq = qs[0].shape[0]
    dv = v_ref.shape[-1]

    def body(j, carry):
        off = pl.multiple_of(j * tk, tk)
        kb = k_ref[0, 0, pl.ds(off, tk), :]
        vb = v_ref[0, 0, pl.ds(off, tk), :]
        new = []
        for s in range(n_sub):
            m_old, l_old, acc = carry[s]
            sc = _dot_nt(qs[s], kb)
            m_new = jnp.maximum(m_old, jnp.max(sc, axis=-1, keepdims=True))
            p = jnp.exp(sc - m_new)
            alpha = jnp.exp(m_old - m_new)
            l_new = alpha * l_old + jnp.sum(p, axis=-1, keepdims=True)
            acc = alpha * acc + _dot(p.astype(BF16), vb)
            new.append((m_new, l_new, acc))
        return tuple(new)

    init = tuple((jnp.full((tq, 1), -jnp.inf, F32), jnp.zeros((tq, 1), F32), jnp.zeros((tq, dv), F32))
                 for _ in range(n_sub))
    res = lax.fori_loop(first, n_all_kv, body, init)
    if n_sub == 1:
        _, l_fin, acc = res[0]
        o_ref[0] = (acc / l_fin).astype(o_ref.dtype)
    else:
        lv = lam_ref[...]
        lam = (jnp.exp(jnp.sum(lv[0:1] * lv[1:2], axis=-1, keepdims=True))
               - jnp.exp(jnp.sum(lv[2:3] * lv[3:4], axis=-1, keepdims=True)) + lam_init)
        o = res[0][2] / res[0][1] - lam * (res[1][2] / res[1][1])
        o = _rms(o, snw_ref[...]) * (1.0 - lam_init)
        o_ref[0] = o.astype(o_ref.dtype)


def _flash_call(q, k, v, seq, n_q_rows, extra=(), lam_init=0.0):
    b, nh, n_sub, nt, dk = q.shape
    dv = v.shape[-1]
    tq, tk = ATTN_TQ, ATTN_TK
    in_specs = [pl.BlockSpec((1, 1, n_sub, tq, dk), lambda bb, hh, i: (bb, hh, 0, i, 0)),
                pl.BlockSpec((1, 1, nt, dk), lambda bb, hh, i: (bb, hh, 0, 0)),
                pl.BlockSpec((1, 1, nt, dv), lambda bb, hh, i: (bb, hh, 0, 0))]
    in_specs += [_const_spec(e.shape) for e in extra]
    return pl.pallas_call(
        functools.partial(_flash_kernel, n_sub=n_sub, tk=tk, n_lat_q=seq // tq, n_lat_kv=seq // tk,
                          n_all_kv=nt // tk, lam_init=lam_init),
        grid=(b, nh, n_q_rows // tq),
        in_specs=in_specs,
        out_specs=pl.BlockSpec((1, tq, dv), lambda bb, hh, i: (bb, i, hh)),
        out_shape=jax.ShapeDtypeStruct((b, n_q_rows, nh * dv), BF16),
        compiler_params=_cparams("parallel", "parallel", "arbitrary"),
        name="flash_attn",
    )(q, k, v, *extra)


def _hg_pre_kernel(x_ref, mod_ref, n1_ref, win_ref, lbl_ref, q_ref, i_ref, g_ref, lf_ref, *, layer):
    m = mod_ref[0, 0]
    a = _norm_mod(x_ref[0], n1_ref[...], m[0:1], m[1:2]).astype(BF16)
    width = N_HEADS * LANES
    t = _dot(a, win_ref[...])
    logits = lbl_ref[...]
    e = jnp.exp(logits - jnp.max(logits, axis=0, keepdims=True))
    sm = e / jnp.sum(e, axis=0, keepdims=True)
    lb = jnp.zeros_like(sm[0])
    for j in range(1, layer + 1):
        lb = lb + sm[j]
    q_ref[0] = t[:, :width]
    i_ref[0] = t[:, 3 * width:4 * width]
    g_ref[0] = t[:, 4 * width:]
    for dirn in range(2):
        z = t[:, (1 + dirn) * width:(2 + dirn) * width]
        lbd = lb[dirn:dirn + 1]
        lf_ref[0, dirn] = jnp.log(lbd + (1.0 - lbd) * jax.nn.sigmoid(z))


def _hg_pre_call(h, mods, layer, n_lat_tiles, n1, w_in, lb_logits):
    b, nt, d = h.shape
    tm = TOKEN_TILE
    weights = [n1, w_in, lb_logits]
    tok = jax.ShapeDtypeStruct((b, nt, d), F32)
    return pl.pallas_call(
        functools.partial(_hg_pre_kernel, layer=layer),
        grid=(b, nt // tm),
        in_specs=[_tok_spec(d), _mod_spec(layer, n_lat_tiles, b, d)] + [_const_spec(w.shape) for w in weights],
        out_specs=[_tok_spec(d), _tok_spec(d), _tok_spec(d),
                   pl.BlockSpec((1, 2, tm, d), lambda bb, i: (bb, 0, i, 0))],
        out_shape=[tok, tok, tok, jax.ShapeDtypeStruct((b, 2, nt, d), F32)],
        compiler_params=_cparams("parallel", "parallel"),
        name="hg_pre",
    )(h, mods, *weights)


def _hg_scan_kernel(q_ref, i_ref, lf_ref, o_ref, st_ref, *, rev):
    @pl.when(pl.program_id(1) == 0)
    def _():
        st_ref[...] = jnp.zeros_like(st_ref)

    q = q_ref[0]
    v = i_ref[0]
    lf = lf_ref[0, 0]
    c = q.shape[0]
    row = lax.broadcasted_iota(jnp.int32, (c, c), 0)
    col = lax.broadcasted_iota(jnp.int32, (c, c), 1)
    tri = jnp.where((col >= row) if rev else (col <= row), 1.0, 0.0).astype(BF16)
    hi = lf.astype(BF16)
    r1 = lf - hi.astype(F32)
    mid = r1.astype(BF16)
    lo = (r1 - mid.astype(F32)).astype(BF16)
    cum = _dot(tri, hi) + _dot(tri, mid) + _dot(tri, lo)
    key = 1.0 - jnp.exp(lf)
    tot = cum[0:1] if rev else cum[c - 1:c]
    q_dec = (q * jnp.exp(cum)).astype(BF16)
    k_dec = (key * jnp.exp(tot - cum)).astype(BF16)
    v16 = v.astype(BF16)
    rows = lax.broadcasted_iota(jnp.int32, (c, 1), 0)
    cols = lax.broadcasted_iota(jnp.int32, (1, HG_SUB), 1)
    for h in range(N_HEADS):
        sl = slice(h * LANES, (h + 1) * LANES)
        st = st_ref[h]
        o = _dot_nt(q_dec[:, sl], st.astype(BF16))
        for jb in range(c // HG_SUB):
            r0 = jb * HG_SUB
            mid_row = r0 + HG_SUB // 2
            ref = cum[mid_row:mid_row + 1, sl]
            kt = (key[r0:r0 + HG_SUB, sl] * jnp.exp(ref - cum[r0:r0 + HG_SUB, sl])).astype(BF16)
            used = (rows < r0 + HG_SUB) if rev else (rows >= r0)
            qt = (q[:, sl] * jnp.exp(jnp.where(used, cum[:, sl] - ref, 0.0))).astype(BF16)
            att = _dot_nt(qt, kt)
            keep = (rows <= r0 + cols) if rev else (rows >= r0 + cols)
            att = jnp.where(keep, att, 0.0).astype(BF16)
            o = o + _dot(att, v16[r0:r0 + HG_SUB, sl])
        o_ref[0, :, sl] = o
        st_ref[h] = st * jnp.exp(tot[:, sl]) + _dot_tn(v16[:, sl], k_dec[:, sl])


def _hg_scan_call(q, i, lf, seq, rev):
    b, nt, d = q.shape
    c = HG_CHUNK
    n_lat = seq // c
    n_ctx = (nt - seq) // c
    dirn = 1 if rev else 0

    def blk(j):
        if rev:
            return n_lat + n_ctx - 1 - j
        return jnp.where(j < n_ctx, n_lat + j, j - n_ctx)

    return pl.pallas_call(
        functools.partial(_hg_scan_kernel, rev=rev),
        grid=(b, nt // c),
        in_specs=[pl.BlockSpec((1, c, d), lambda bb, j: (bb, blk(j), 0)),
                  pl.BlockSpec((1, c, d), lambda bb, j: (bb, blk(j), 0)),
                  pl.BlockSpec((1, 1, c, d), lambda bb, j: (bb, dirn, blk(j), 0))],
        out_specs=pl.BlockSpec((1, c, d), lambda bb, j: (bb, blk(j), 0)),
        out_shape=jax.ShapeDtypeStruct((b, nt, d), F32),
        scratch_shapes=[pltpu.VMEM((N_HEADS, LANES, LANES), F32)],
        compiler_params=_cparams("parallel", "arbitrary"),
        name="hg_scan_rev" if rev else "hg_scan_fwd",
    )(q, i, lf)


def _post_kernel(*refs, mode, d_ff):
    if mode == "hg":
        x_ref, mod_ref, of_ref, ob_ref, g_ref, onw_ref, wo_ref, n2_ref, w13_ref, w2_ref, out_ref = refs
        o = _group_rms(of_ref[0] + ob_ref[0], onw_ref[...], LANES) * _silu(g_ref[0])
        o = o.astype(BF16)
    else:
        x_ref, mod_ref, o_ref, wo_ref, n2_ref, w13_ref, w2_ref, out_ref = refs
        o = o_ref[0]
    m = mod_ref[0, 0]
    x1 = x_ref[0] + m[2:3] * _dot(o, wo_ref[...])
    a2 = _norm_mod(x1, n2_ref[...], m[3:4], m[4:5]).astype(BF16)
    hc = _dot(a2, w13_ref[...])
    hid = (_silu(hc[:, :d_ff]) * hc[:, d_ff:]).astype(BF16)
    out_ref[0] = x1 + m[5:6] * _dot(hid, w2_ref[...])


def _post_call(h, mods, layer, n_lat_tiles, n_out_rows, mixer_out, w_o, n2, w13, w2, onw=None):
    b, nt, d = h.shape
    tm = TOKEN_TILE
    d_ff = w2.shape[0]
    mode = "hg" if onw is not None else "plain"
    acts = list(mixer_out)
    weights = ([onw] if mode == "hg" else []) + [w_o, n2, w13, w2]
    return pl.pallas_call(
        functools.partial(_post_kernel, mode=mode, d_ff=d_ff),
        grid=(b, n_out_rows // tm),
        in_specs=[_tok_spec(d), _mod_spec(layer, n_lat_tiles, b, d)]
        + [_tok_spec(d) for _ in acts] + [_const_spec(w.shape) for w in weights],
        out_specs=_tok_spec(d),
        out_shape=jax.ShapeDtypeStruct((b, n_out_rows, d), F32),
        compiler_params=_cparams("parallel", "parallel"),
        name="post_" + mode,
    )(h, mods, *acts, *weights)


def _rope_tables(seq, ctx_len, reps):
    rows = seq // GRID_W
    row = jnp.repeat(jnp.arange(rows, dtype=F32), GRID_W)
    col = jnp.tile(jnp.arange(GRID_W, dtype=F32), rows)
    axis_dim = ROPE_DIM // 2
    inv_freq = jnp.power(ROPE_BASE, -jnp.arange(0, axis_dim, 2, dtype=F32) / axis_dim)
    ang_r = row[:, None] * inv_freq
    ang_c = col[:, None] * inv_freq
    ang = jnp.concatenate([ang_r, ang_r, ang_c, ang_c], axis=-1)
    cos = jnp.concatenate([jnp.cos(ang), jnp.ones((ctx_len, ROPE_DIM), F32)], axis=0)
    sin = jnp.concatenate([jnp.sin(ang), jnp.zeros((ctx_len, ROPE_DIM), F32)], axis=0)
    return jnp.tile(cos, (1, reps)), jnp.tile(sin, (1, reps))


def _row(w, reps=1):
    return jnp.tile(w.astype(F32), reps)[None, :]


def _pad_groups(w, group, n_groups, to):
    lead = w.shape[:-1]
    w = w.reshape(lead + (n_groups, group))
    w = jnp.pad(w, [(0, 0)] * len(lead) + [(0, 0), (0, to - group)])
    return w.reshape(lead + (n_groups * to,))


def _mla_params(j, mla_w_in, mla_q_norm_w, mla_kv_norm_w, mla_w_uq, mla_w_ukv, mla_qn_w, mla_qr_w,
                mla_kn_w, mla_kr_w):
    hq = MLA_NOPE + MLA_ROPE
    w_uq = mla_w_uq[j].reshape(MLA_Q_RANK, N_HEADS, hq)
    w_ukv = mla_w_ukv[j].reshape(MLA_KV_RANK, N_HEADS, MLA_NOPE + MLA_V)
    return {
        "w_in": jnp.pad(mla_w_in[j], ((0, 0), (0, LANES - MLA_ROPE))).astype(BF16),
        "q_norm_w": _row(mla_q_norm_w[j]),
        "kv_norm_w": _row(mla_kv_norm_w[j]),
        "w_uq_n": w_uq[:, :, :MLA_NOPE].reshape(MLA_Q_RANK, -1).astype(BF16),
        "w_uq_r": _pad_groups(w_uq[:, :, MLA_NOPE:].reshape(MLA_Q_RANK, -1), MLA_ROPE, N_HEADS,
                              LANES).astype(BF16),
        "w_uk": w_ukv[:, :, :MLA_NOPE].reshape(MLA_KV_RANK, -1).astype(BF16),
        "w_uv": w_ukv[:, :, MLA_NOPE:].reshape(MLA_KV_RANK, -1).astype(BF16),
        "qn_w": _row(mla_qn_w[j], N_HEADS),
        "qr_w": _row(_pad_groups(mla_qr_w[j], MLA_ROPE, 1, LANES), N_HEADS),
        "kn_w": _row(mla_kn_w[j], N_HEADS),
        "kr_w": _row(_pad_groups(mla_kr_w[j], MLA_ROPE, 1, LANES)),
    }


def kernel(x, c, ctx, c_ctx, ada_w, ada_b, norm1_w, norm2_w, ffn_w1, ffn_w3, ffn_w2, mla_w_in, mla_q_norm_w, mla_kv_norm_w, mla_w_uq, mla_w_ukv, mla_qn_w, mla_qr_w, mla_kn_w, mla_kr_w, mla_w_o, hg_w_in, hg_lb_logits, hg_o_norm_w, hg_w_o, df_w_qkv, df_qn_w, df_kn_w, df_lambda, df_sub_norm_w, df_w_o):
    b, seq, d = x.shape
    ctx_len = ctx.shape[1]
    depth = ada_w.shape[0]
    tm = TOKEN_TILE
    assert d == N_HEADS * LANES and b < 8
    assert seq % tm == 0 and ctx_len % tm == 0 and seq % GRID_W == 0
    assert seq % ATTN_TK == 0 and ctx_len % ATTN_TK == 0 and ctx_len % HG_CHUNK == 0
    nt = seq + ctx_len
    n_lat_tiles = seq // tm

    h = jnp.concatenate([x, ctx], axis=1)
    cvec = jnp.zeros((8, d), F32).at[:b].set(c).at[b].set(c_ctx)
    mods = _ada_call(cvec, ada_w, ada_b).reshape(depth, 8, N_MOD, d)
    cos_t, sin_t = _rope_tables(seq, ctx_len, d // ROPE_DIM)

    for i in range(depth):
        kind, j = i % N_MIXERS, i // N_MIXERS
        last = i == depth - 1
        n_out_rows = seq if last else nt
        n1 = _row(norm1_w[i])
        n2 = _row(norm2_w[i])
        w13 = jnp.concatenate([ffn_w1[i], ffn_w3[i]], axis=1).astype(BF16)
        w2 = ffn_w2[i].astype(BF16)
        onw = None
        if kind == 0:
            p = _mla_params(j, mla_w_in, mla_q_norm_w, mla_kv_norm_w, mla_w_uq, mla_w_ukv, mla_qn_w,
                            mla_qr_w, mla_kn_w, mla_kr_w)
            q, k, v = _mla_pre_call(h, mods, i, n_lat_tiles, n1, p, cos_t, sin_t)
            mixer_out = [_flash_call(q[:, :, None], k, v, seq, n_out_rows)]
            w_o = mla_w_o[j].astype(BF16)
        elif kind == 1:
            q, gi, g, lf = _hg_pre_call(h, mods, i, n_lat_tiles, n1, hg_w_in[j].astype(BF16),
                                        hg_lb_logits.astype(F32))
            mixer_out = [_hg_scan_call(q, gi, lf, seq, False), _hg_scan_call(q, gi, lf, seq, True), g]
            onw = _row(hg_o_norm_w[j], N_HEADS)
            w_o = hg_w_o[j].astype(BF16)
        else:
            p = {"w_qkv": df_w_qkv[j].astype(BF16), "qn_w": _row(df_qn_w[j], d // ROPE_DIM),
                 "kn_w": _row(df_kn_w[j], d // ROPE_DIM)}
            q, k, v = _df_pre_call(h, mods, i, n_lat_tiles, n1, p, cos_t, sin_t)
            lam_init = 0.8 - 0.6 * math.exp(-0.3 * i)
            mixer_out = [_flash_call(q, k, v, seq, n_out_rows,
                                     extra=(df_lambda[j].astype(F32), _row(df_sub_norm_w[j])),
                                     lam_init=lam_init)]
            w_o = df_w_o[j].astype(BF16)
        h = _post_call(h, mods, i, n_lat_tiles, n_out_rows, mixer_out, w_o, n2, w13, w2, onw)
    return h
```

```python
import functools
import math

import jax
import jax.numpy as jnp
from jax import lax
from jax.experimental import pallas as pl
from jax.experimental.pallas import tpu as pltpu

F32 = jnp.float32
BF16 = jnp.bfloat16

EPS = 1e-6
ROPE_BASE = 10000.0
GRID_W = 64
N_MOD = 6
N_MIXERS = 3

N_HEADS = 8
LANES = 128
MLA_NOPE = 128
MLA_ROPE = 64
MLA_V = 128
MLA_Q_RANK = 384
MLA_KV_RANK = 256
MLA_DK = 256
ROPE_DIM = 64
ROPE_QUARTER = ROPE_DIM // 4

TOKEN_TILE = 256
ATTN_TQ = 512
ATTN_TKB = 2048
HG_CHUNK = 64
HG_SUB = 16
VMEM_LIMIT_BYTES = 56 * 1024 * 1024


def _cparams(*sem):
    return pltpu.CompilerParams(dimension_semantics=sem, vmem_limit_bytes=VMEM_LIMIT_BYTES)


def _const_spec(shape):
    nd = len(shape)
    return pl.BlockSpec(shape, lambda *_: (0,) * nd, pipeline_mode=pl.Buffered(1))


def _dot(a, b):
    return jnp.dot(a, b, preferred_element_type=F32)


def _dot_nt(a, b):
    return lax.dot_general(a, b, (((1,), (1,)), ((), ())), preferred_element_type=F32)


def _dot_tn(a, b):
    return lax.dot_general(a, b, (((0,), (0,)), ((), ())), preferred_element_type=F32)


def _silu(t):
    return t * jax.nn.sigmoid(t)


def _rms(t, w):
    return t * lax.rsqrt(jnp.mean(t * t, axis=-1, keepdims=True) + EPS) * w


def _norm_mod(t, w, shift, scale):
    return _rms(t, w) * (1.0 + scale) + shift


def _group_rms(t, w, group):
    width = t.shape[-1]
    lane = lax.broadcasted_iota(jnp.int32, (1, LANES), 1)
    low = lane < 64
    outs = []
    for p in range(width // LANES):
        ts = t[:, p * LANES:(p + 1) * LANES]
        sq = ts * ts
        if group == LANES:
            ms = jnp.sum(sq, axis=-1, keepdims=True) * (1.0 / LANES)
        else:
            s_lo = jnp.sum(jnp.where(low, sq, 0.0), axis=-1, keepdims=True)
            s_hi = jnp.sum(jnp.where(low, 0.0, sq), axis=-1, keepdims=True)
            ms = jnp.where(low, s_lo, s_hi) * (1.0 / 64)
        outs.append(ts * lax.rsqrt(ms + EPS))
    return jnp.concatenate(outs, axis=-1) * w


def _rope(t, cos, sin):
    width = t.shape[-1]
    lane = lax.broadcasted_iota(jnp.int32, (1, width), 1)
    up = pltpu.roll(t, width - ROPE_QUARTER, 1)
    dn = pltpu.roll(t, ROPE_QUARTER, 1)
    rot = jnp.where((lane % (2 * ROPE_QUARTER)) < ROPE_QUARTER, -up, dn)
    return t * cos + rot * sin


def _ada_kernel(c_ref, w_ref, b_ref, o_ref):
    s = _silu(c_ref[...]).astype(BF16)
    o_ref[0] = _dot(s, w_ref[0].astype(BF16)) + b_ref[0]


def _ada_call(cvec, ada_w, ada_b):
    depth, d, n = ada_w.shape
    tn = 768
    return pl.pallas_call(
        _ada_kernel,
        grid=(depth, n // tn),
        in_specs=[
            pl.BlockSpec((8, d), lambda l, j: (0, 0)),
            pl.BlockSpec((1, d, tn), lambda l, j: (l, 0, j)),
            pl.BlockSpec((1, 1, tn), lambda l, j: (l, 0, j)),
        ],
        out_specs=pl.BlockSpec((1, 8, tn), lambda l, j: (l, 0, j)),
        out_shape=jax.ShapeDtypeStruct((depth, 8, n), F32),
        compiler_params=_cparams("parallel", "parallel"),
        name="ada_mod",
    )(cvec, ada_w, ada_b.reshape(depth, 1, n))


def _tok_spec(width, tm=TOKEN_TILE):
    return pl.BlockSpec((1, tm, width), lambda b, i: (b, i, 0))


def _mod_spec(layer, n_lat_tiles, ctx_row, d):
    return pl.BlockSpec((1, 1, N_MOD, d),
                        lambda b, i: (layer, jnp.where(i < n_lat_tiles, b, ctx_row), 0, 0))


def _head_spec(width, tm=TOKEN_TILE):
    return pl.BlockSpec((1, N_HEADS, tm, width), lambda b, i: (b, 0, i, 0))


def _mla_pre_kernel(x_ref, mod_ref, n1_ref, win_ref, qnw_ref, kvnw_ref, wuqn_ref, wuqr_ref,
                    wk_ref, wv_ref, qn_ref, qr_ref, kn_ref, kr_ref, cos_ref, sin_ref,
                    q_ref, k_ref, v_ref, *, scale):
    m = mod_ref[0, 0]
    a = _norm_mod(x_ref[0], n1_ref[...], m[0:1], m[1:2]).astype(BF16)
    t = _dot(a, win_ref[...])
    cq = _rms(t[:, :MLA_Q_RANK], qnw_ref[...]).astype(BF16)
    ckv = _rms(t[:, MLA_Q_RANK:MLA_Q_RANK + MLA_KV_RANK], kvnw_ref[...]).astype(BF16)
    kr = t[:, MLA_Q_RANK + MLA_KV_RANK:]
    cos = cos_ref[...]
    sin = sin_ref[...]
    kr = kr * lax.rsqrt(jnp.sum(kr * kr, axis=-1, keepdims=True) * (1.0 / MLA_ROPE) + EPS) * kr_ref[...]
    kr = _rope(kr, cos[:, :LANES], sin[:, :LANES]).astype(BF16)
    qn = _group_rms(_dot(cq, wuqn_ref[...]), qn_ref[...], LANES) * scale
    qr = _group_rms(_dot(cq, wuqr_ref[...]), qr_ref[...], 64)
    qr = _rope(qr, cos, sin) * scale
    kn = _group_rms(_dot(ckv, wk_ref[...]), kn_ref[...], LANES)
    v = _dot(ckv, wv_ref[...])
    for h in range(N_HEADS):
        sl = slice(h * LANES, (h + 1) * LANES)
        q_ref[0, h, :, :LANES] = qn[:, sl].astype(BF16)
        q_ref[0, h, :, LANES:] = qr[:, sl].astype(BF16)
        k_ref[0, h, :, :LANES] = kn[:, sl].astype(BF16)
        k_ref[0, h, :, LANES:] = kr
        v_ref[0, h] = v[:, sl].astype(BF16)


def _mla_pre_call(h, mods, layer, n_lat_tiles, n1, p, cos_t, sin_t):
    b, nt, d = h.shape
    tm = TOKEN_TILE
    width = N_HEADS * LANES
    weights = [n1, p["w_in"], p["q_norm_w"], p["kv_norm_w"], p["w_uq_n"], p["w_uq_r"], p["w_uk"],
               p["w_uv"], p["qn_w"], p["qr_w"], p["kn_w"], p["kr_w"]]
    scale = 1.0 / math.sqrt(MLA_NOPE + MLA_ROPE)
    return pl.pallas_call(
        functools.partial(_mla_pre_kernel, scale=scale),
        grid=(b, nt // tm),
        in_specs=[_tok_spec(d), _mod_spec(layer, n_lat_tiles, b, d)]
        + [_const_spec(w.shape) for w in weights]
        + [pl.BlockSpec((tm, width), lambda bb, i: (i, 0))] * 2,
        out_specs=[_head_spec(MLA_DK), _head_spec(MLA_DK), _head_spec(MLA_V)],
        out_shape=[jax.ShapeDtypeStruct((b, N_HEADS, nt, MLA_DK), BF16),
                   jax.ShapeDtypeStruct((b, N_HEADS, nt, MLA_DK), BF16),
                   jax.ShapeDtypeStruct((b, N_HEADS, nt, MLA_V), BF16)],
        compiler_params=_cparams("parallel", "parallel"),
        name="mla_pre",
    )(h, mods, *weights, cos_t, sin_t)


def _df_pre_kernel(x_ref, mod_ref, n1_ref, wqkv_ref, qn_ref, kn_ref, cos_ref, sin_ref,
                   q_ref, k_ref, v_ref, *, scale):
    m = mod_ref[0, 0]
    a = _norm_mod(x_ref[0], n1_ref[...], m[0:1], m[1:2]).astype(BF16)
    width = N_HEADS * LANES
    t = _dot(a, wqkv_ref[...])
    cos = cos_ref[...]
    sin = sin_ref[...]
    q = _rope(_group_rms(t[:, :width], qn_ref[...], 64), cos, sin) * scale
    k = _rope(_group_rms(t[:, width:2 * width], kn_ref[...], 64), cos, sin)
    v = t[:, 2 * width:]
    lane = lax.broadcasted_iota(jnp.int32, (1, LANES), 1)
    low = lane < 64
    for h in range(N_HEADS):
        sl = slice(h * LANES, (h + 1) * LANES)
        qh = q[:, sl]
        q_ref[0, h, 0] = jnp.where(low, qh, 0.0).astype(BF16)
        q_ref[0, h, 1] = jnp.where(low, 0.0, qh).astype(BF16)
        k_ref[0, h] = k[:, sl].astype(BF16)
        v_ref[0, h] = v[:, sl].astype(BF16)


def _df_pre_call(h, mods, layer, n_lat_tiles, n1, p, cos_t, sin_t):
    b, nt, d = h.shape
    tm = TOKEN_TILE
    width = N_HEADS * LANES
    weights = [n1, p["w_qkv"], p["qn_w"], p["kn_w"]]
    return pl.pallas_call(
        functools.partial(_df_pre_kernel, scale=1.0 / math.sqrt(ROPE_DIM)),
        grid=(b, nt // tm),
        in_specs=[_tok_spec(d), _mod_spec(layer, n_lat_tiles, b, d)]
        + [_const_spec(w.shape) for w in weights]
        + [pl.BlockSpec((tm, width), lambda bb, i: (i, 0))] * 2,
        out_specs=[pl.BlockSpec((1, N_HEADS, 2, tm, LANES), lambda bb, i: (bb, 0, 0, i, 0)),
                   _head_spec(LANES), _head_spec(LANES)],
        out_shape=[jax.ShapeDtypeStruct((b, N_HEADS, 2, nt, LANES), BF16),
                   jax.ShapeDtypeStruct((b, N_HEADS, nt, LANES), BF16),
                   jax.ShapeDtypeStruct((b, N_HEADS, nt, LANES), BF16)],
        compiler_params=_cparams("parallel", "parallel"),
        name="df_pre",
    )(h, mods, *weights, cos_t, sin_t)


def _flash_kernel(*refs, n_sub, head_blocks, n_loop, tkb, lam_init):
    if n_sub == 2:
        q_ref, k_ref, v_ref, lam_ref, snw_ref, o_ref = refs
    else:
        q_ref, k_ref, v_ref, o_ref = refs
    qs = [q_ref[0, 0, s] for s in range(n_sub)]
    tq = qs[0].shape[0]
    dv = v_ref.shape[-1]

    def step(carry, kb, vb):
        new = []
        for s in range(n_sub):
            m_old, l_old, acc = carry[s]
            sc = _dot_nt(qs[s], kb)
            m_new = jnp.maximum(m_old, jnp.max(sc, axis=-1, keepdims=True))
            p = jnp.exp(sc - m_new)
            alpha = jnp.exp(m_old - m_new)
            l_new = alpha * l_old + jnp.sum(p, axis=-1, keepdims=True)
            acc = alpha * acc + _dot(p.astype(BF16), vb)
            new.append((m_new, l_new, acc))
        return tuple(new)

    carry = tuple((jnp.full((tq, 1), -jnp.inf, F32), jnp.zeros((tq, 1), F32), jnp.zeros((tq, dv), F32))
                  for _ in range(n_sub))
    for start, rows in head_blocks:
        carry = step(carry, k_ref[0, 0, start:start + rows, :], v_ref[0, 0, start:start + rows, :])
    if n_loop:
        def body(j, c):
            off = pl.multiple_of(j * tkb, tkb)
            return step(c, k_ref[0, 0, pl.ds(off, tkb), :], v_ref[0, 0, pl.ds(off, tkb), :])
        carry = lax.fori_loop(0, n_loop, body, carry, unroll=True)
    if n_sub == 1:
        _, l_fin, acc = carry[0]
        o_ref[0] = (acc / l_fin).astype(o_ref.dtype)
    else:
        lv = lam_ref[...]
        lam = (jnp.exp(jnp.sum(lv[0:1] * lv[1:2], axis=-1, keepdims=True))
               - jnp.exp(jnp.sum(lv[2:3] * lv[3:4], axis=-1, keepdims=True)) + lam_init)
        o = carry[0][2] / carry[0][1] - lam * (carry[1][2] / carry[1][1])
        o = _rms(o, snw_ref[...]) * (1.0 - lam_init)
        o_ref[0] = o.astype(o_ref.dtype)


def _flash_call(q, k, v, seq, ctx_queries, extra=(), lam_init=0.0):
    b, nh, n_sub, nt, dk = q.shape
    dv = v.shape[-1]
    ctx_len = nt - seq
    if ctx_queries:
        tq, n_rows, q0 = ctx_len, ctx_len, seq // ctx_len
        kv_rows, kv0 = ctx_len, seq // ctx_len
        head_blocks, n_loop, tkb = ((0, ctx_len),), 0, 0
    else:
        tq, n_rows, q0 = min(ATTN_TQ, seq), seq, 0
        kv_rows, kv0 = nt, 0
        tkb = min(ATTN_TKB, seq)
        head_blocks, n_loop = ((seq, ctx_len),), seq // tkb
    in_specs = [pl.BlockSpec((1, 1, n_sub, tq, dk), lambda bb, hh, i: (bb, hh, 0, q0 + i, 0)),
                pl.BlockSpec((1, 1, kv_rows, dk), lambda bb, hh, i: (bb, hh, kv0, 0)),
                pl.BlockSpec((1, 1, kv_rows, dv), lambda bb, hh, i: (bb, hh, kv0, 0))]
    in_specs += [_const_spec(e.shape) for e in extra]
    return pl.pallas_call(
        functools.partial(_flash_kernel, n_sub=n_sub, head_blocks=head_blocks, n_loop=n_loop, tkb=tkb,
                          lam_init=lam_init),
        grid=(b, nh, n_rows // tq),
        in_specs=in_specs,
        out_specs=pl.BlockSpec((1, tq, dv), lambda bb, hh, i: (bb, i, hh)),
        out_shape=jax.ShapeDtypeStruct((b, n_rows, nh * dv), BF16),
        compiler_params=_cparams("parallel", "parallel", "arbitrary"),
        name="flash_ctx" if ctx_queries else "flash_lat",
    )(q, k, v, *extra)


def _attend(q, k, v, seq, with_ctx, extra=(), lam_init=0.0):
    o = _flash_call(q, k, v, seq, False, extra, lam_init)
    if with_ctx:
        o = jnp.concatenate([o, _flash_call(q, k, v, seq, True, extra, lam_init)], axis=1)
    return o


def _hg_pre_kernel(x_ref, mod_ref, n1_ref, win_ref, lbl_ref, q_ref, i_ref, g_ref, lf_ref, *, layer):
    m = mod_ref[0, 0]
    a = _norm_mod(x_ref[0], n1_ref[...], m[0:1], m[1:2]).astype(BF16)
    width = N_HEADS * LANES
    t = _dot(a, win_ref[...])
    logits = lbl_ref[...]
    e = jnp.exp(logits - jnp.max(logits, axis=0, keepdims=True))
    sm = e / jnp.sum(e, axis=0, keepdims=True)
    lb = jnp.zeros_like(sm[0])
    for j in range(1, layer + 1):
        lb = lb + sm[j]
    q_ref[0] = t[:, :width]
    i_ref[0] = t[:, 3 * width:4 * width]
    g_ref[0] = t[:, 4 * width:]
    for dirn in range(2):
        z = t[:, (1 + dirn) * width:(2 + dirn) * width]
        lbd = lb[dirn:dirn + 1]
        lf_ref[0, dirn] = jnp.log(lbd + (1.0 - lbd) * jax.nn.sigmoid(z))


def _hg_pre_call(h, mods, layer, n_lat_tiles, n1, w_in, lb_logits):
    b, nt, d = h.shape
    tm = TOKEN_TILE
    weights = [n1, w_in, lb_logits]
    tok = jax.ShapeDtypeStruct((b, nt, d), F32)
    return pl.pallas_call(
        functools.partial(_hg_pre_kernel, layer=layer),
        grid=(b, nt // tm),
        in_specs=[_tok_spec(d), _mod_spec(layer, n_lat_tiles, b, d)] + [_const_spec(w.shape) for w in weights],
        out_specs=[_tok_spec(d), _tok_spec(d), _tok_spec(d),
                   pl.BlockSpec((1, 2, tm, d), lambda bb, i: (bb, 0, i, 0))],
        out_shape=[tok, tok, tok, jax.ShapeDtypeStruct((b, 2, nt, d), F32)],
        compiler_params=_cparams("parallel", "parallel"),
        name="hg_pre",
    )(h, mods, *weights)


def _hg_scan_kernel(q_ref, i_ref, lf_ref, o_ref, st_ref, *, rev):
    @pl.when(pl.program_id(1) == 0)
    def _():
        st_ref[...] = jnp.zeros_like(st_ref)

    q = q_ref[0]
    v = i_ref[0]
    lf = lf_ref[0, 0]
    c = q.shape[0]
    row = lax.broadcasted_iota(jnp.int32, (c, c), 0)
    col = lax.broadcasted_iota(jnp.int32, (c, c), 1)
    tri = jnp.where((col >= row) if rev else (col <= row), 1.0, 0.0).astype(BF16)
    hi = lf.astype(BF16)
    r1 = lf - hi.astype(F32)
    mid = r1.astype(BF16)
    lo = (r1 - mid.astype(F32)).astype(BF16)
    cum = _dot(tri, hi) + _dot(tri, mid) + _dot(tri, lo)
    key = 1.0 - jnp.exp(lf)
    tot = cum[0:1] if rev else cum[c - 1:c]
    q_dec = (q * jnp.exp(cum)).astype(BF16)
    k_dec = (key * jnp.exp(tot - cum)).astype(BF16)
    v16 = v.astype(BF16)
    rows = lax.broadcasted_iota(jnp.int32, (c, 1), 0)
    cols = lax.broadcasted_iota(jnp.int32, (1, HG_SUB), 1)
    for h in range(N_HEADS):
        sl = slice(h * LANES, (h + 1) * LANES)
        st = st_ref[h]
        o = _dot_nt(q_dec[:, sl], st.astype(BF16))
        for jb in range(c // HG_SUB):
            r0 = jb * HG_SUB
            mid_row = r0 + HG_SUB // 2
            ref = cum[mid_row:mid_row + 1, sl]
            kt = (key[r0:r0 + HG_SUB, sl] * jnp.exp(ref - cum[r0:r0 + HG_SUB, sl])).astype(BF16)
            used = (rows < r0 + HG_SUB) if rev else (rows >= r0)
            qt = (q[:, sl] * jnp.exp(jnp.where(used, cum[:, sl] - ref, 0.0))).astype(BF16)
            att = _dot_nt(qt, kt)
            keep = (rows <= r0 + cols) if rev else (rows >= r0 + cols)
            att = jnp.where(keep, att, 0.0).astype(BF16)
            o = o + _dot(att, v16[r0:r0 + HG_SUB, sl])
        o_ref[0, :, sl] = o
        st_ref[h] = st * jnp.exp(tot[:, sl]) + _dot_tn(v16[:, sl], k_dec[:, sl])


def _hg_scan_call(q, i, lf, seq, rev):
    b, nt, d = q.shape
    c = HG_CHUNK
    n_lat = seq // c
    n_ctx = (nt - seq) // c
    dirn = 1 if rev else 0

    def blk(j):
        if rev:
            return n_lat + n_ctx - 1 - j
        return jnp.where(j < n_ctx, n_lat + j, j - n_ctx)

    return pl.pallas_call(
        functools.partial(_hg_scan_kernel, rev=rev),
        grid=(b, nt // c),
        in_specs=[pl.BlockSpec((1, c, d), lambda bb, j: (bb, blk(j), 0)),
                  pl.BlockSpec((1, c, d), lambda bb, j: (bb, blk(j), 0)),
                  pl.BlockSpec((1, 1, c, d), lambda bb, j: (bb, dirn, blk(j), 0))],
        out_specs=pl.BlockSpec((1, c, d), lambda bb, j: (bb, blk(j), 0)),
        out_shape=jax.ShapeDtypeStruct((b, nt, d), F32),
        scratch_shapes=[pltpu.VMEM((N_HEADS, LANES, LANES), F32)],
        compiler_params=_cparams("parallel", "arbitrary"),
        name="hg_scan_rev" if rev else "hg_scan_fwd",
    )(q, i, lf)


def _post_kernel(*refs, mode, d_ff):
    if mode == "hg":
        x_ref, mod_ref, of_ref, ob_ref, g_ref, onw_ref, wo_ref, n2_ref, w13_ref, w2_ref, out_ref = refs
        o = _group_rms(of_ref[0] + ob_ref[0], onw_ref[...], LANES) * _silu(g_ref[0])
        o = o.astype(BF16)
    else:
        x_ref, mod_ref, o_ref, wo_ref, n2_ref, w13_ref, w2_ref, out_ref = refs
        o = o_ref[0]
    m = mod_ref[0, 0]
    x1 = x_ref[0] + m[2:3] * _dot(o, wo_ref[...])
    a2 = _norm_mod(x1, n2_ref[...], m[3:4], m[4:5]).astype(BF16)
    hc = _dot(a2, w13_ref[...])
    hid = (_silu(hc[:, :d_ff]) * hc[:, d_ff:]).astype(BF16)
    out_ref[0] = x1 + m[5:6] * _dot(hid, w2_ref[...])


def _post_call(h, mods, layer, n_lat_tiles, n_out_rows, mixer_out, w_o, n2, w13, w2, onw=None):
    b, nt, d = h.shape
    tm = TOKEN_TILE
    d_ff = w2.shape[0]
    mode = "hg" if onw is not None else "plain"
    acts = list(mixer_out)
    weights = ([onw] if mode == "hg" else []) + [w_o, n2, w13, w2]
    return pl.pallas_call(
        functools.partial(_post_kernel, mode=mode, d_ff=d_ff),
        grid=(b, n_out_rows // tm),
        in_specs=[_tok_spec(d), _mod_spec(layer, n_lat_tiles, b, d)]
        + [_tok_spec(d) for _ in acts] + [_const_spec(w.shape) for w in weights],
        out_specs=_tok_spec(d),
        out_shape=jax.ShapeDtypeStruct((b, n_out_rows, d), F32),
        compiler_params=_cparams("parallel", "parallel"),
        name="post_" + mode,
    )(h, mods, *acts, *weights)


def _rope_tables(seq, ctx_len, reps):
    rows = seq // GRID_W
    row = jnp.repeat(jnp.arange(rows, dtype=F32), GRID_W)
    col = jnp.tile(jnp.arange(GRID_W, dtype=F32), rows)
    axis_dim = ROPE_DIM // 2
    inv_freq = jnp.power(ROPE_BASE, -jnp.arange(0, axis_dim, 2, dtype=F32) / axis_dim)
    ang_r = row[:, None] * inv_freq
    ang_c = col[:, None] * inv_freq
    ang = jnp.concatenate([ang_r, ang_r, ang_c, ang_c], axis=-1)
    cos = jnp.concatenate([jnp.cos(ang), jnp.ones((ctx_len, ROPE_DIM), F32)], axis=0)
    sin = jnp.concatenate([jnp.sin(ang), jnp.zeros((ctx_len, ROPE_DIM), F32)], axis=0)
    return jnp.tile(cos, (1, reps)), jnp.tile(sin, (1, reps))


def _row(w, reps=1):
    return jnp.tile(w.astype(F32), reps)[None, :]


def _pad_groups(w, group, n_groups, to):
    lead = w.shape[:-1]
    w = w.reshape(lead + (n_groups, group))
    w = jnp.pad(w, [(0, 0)] * len(lead) + [(0, 0), (0, to - group)])
    return w.reshape(lead + (n_groups * to,))


def _mla_params(j, mla_w_in, mla_q_norm_w, mla_kv_norm_w, mla_w_uq, mla_w_ukv, mla_qn_w, mla_qr_w,
                mla_kn_w, mla_kr_w):
    hq = MLA_NOPE + MLA_ROPE
    w_uq = mla_w_uq[j].reshape(MLA_Q_RANK, N_HEADS, hq)
    w_ukv = mla_w_ukv[j].reshape(MLA_KV_RANK, N_HEADS, MLA_NOPE + MLA_V)
    return {
        "w_in": jnp.pad(mla_w_in[j], ((0, 0), (0, LANES - MLA_ROPE))).astype(BF16),
        "q_norm_w": _row(mla_q_norm_w[j]),
        "kv_norm_w": _row(mla_kv_norm_w[j]),
        "w_uq_n": w_uq[:, :, :MLA_NOPE].reshape(MLA_Q_RANK, -1).astype(BF16),
        "w_uq_r": _pad_groups(w_uq[:, :, MLA_NOPE:].reshape(MLA_Q_RANK, -1), MLA_ROPE, N_HEADS,
                              LANES).astype(BF16),
        "w_uk": w_ukv[:, :, :MLA_NOPE].reshape(MLA_KV_RANK, -1).astype(BF16),
        "w_uv": w_ukv[:, :, MLA_NOPE:].reshape(MLA_KV_RANK, -1).astype(BF16),
        "qn_w": _row(mla_qn_w[j], N_HEADS),
        "qr_w": _row(_pad_groups(mla_qr_w[j], MLA_ROPE, 1, LANES), N_HEADS),
        "kn_w": _row(mla_kn_w[j], N_HEADS),
        "kr_w": _row(_pad_groups(mla_kr_w[j], MLA_ROPE, 1, LANES)),
    }


def kernel(x, c, ctx, c_ctx, ada_w, ada_b, norm1_w, norm2_w, ffn_w1, ffn_w3, ffn_w2, mla_w_in, mla_q_norm_w, mla_kv_norm_w, mla_w_uq, mla_w_ukv, mla_qn_w, mla_qr_w, mla_kn_w, mla_kr_w, mla_w_o, hg_w_in, hg_lb_logits, hg_o_norm_w, hg_w_o, df_w_qkv, df_qn_w, df_kn_w, df_lambda, df_sub_norm_w, df_w_o):
    b, seq, d = x.shape
    ctx_len = ctx.shape[1]
    depth = ada_w.shape[0]
    tm = TOKEN_TILE
    assert d == N_HEADS * LANES and b < 8
    assert seq % tm == 0 and ctx_len % tm == 0 and seq % GRID_W == 0
    assert seq % ctx_len == 0 and seq % min(ATTN_TQ, seq) == 0 and seq % min(ATTN_TKB, seq) == 0
    assert ctx_len % HG_CHUNK == 0 and seq % HG_CHUNK == 0
    nt = seq + ctx_len
    n_lat_tiles = seq // tm

    h = jnp.concatenate([x, ctx], axis=1)
    cvec = jnp.zeros((8, d), F32).at[:b].set(c).at[b].set(c_ctx)
    mods = _ada_call(cvec, ada_w, ada_b).reshape(depth, 8, N_MOD, d)
    cos_t, sin_t = _rope_tables(seq, ctx_len, d // ROPE_DIM)

    for i in range(depth):
        kind, j = i % N_MIXERS, i // N_MIXERS
        last = i == depth - 1
        n_out_rows = seq if last else nt
        n1 = _row(norm1_w[i])
        n2 = _row(norm2_w[i])
        w13 = jnp.concatenate([ffn_w1[i], ffn_w3[i]], axis=1).astype(BF16)
        w2 = ffn_w2[i].astype(BF16)
        onw = None
        if kind == 0:
            p = _mla_params(j, mla_w_in, mla_q_norm_w, mla_kv_norm_w, mla_w_uq, mla_w_ukv, mla_qn_w,
                            mla_qr_w, mla_kn_w, mla_kr_w)
            q, k, v = _mla_pre_call(h, mods, i, n_lat_tiles, n1, p, cos_t, sin_t)
            mixer_out = [_attend(q[:, :, None], k, v, seq, not last)]
            w_o = mla_w_o[j].astype(BF16)
        elif kind == 1:
            q, gi, g, lf = _hg_pre_call(h, mods, i, n_lat_tiles, n1, hg_w_in[j].astype(BF16),
                                        hg_lb_logits.astype(F32))
            mixer_out = [_hg_scan_call(q, gi, lf, seq, False), _hg_scan_call(q, gi, lf, seq, True), g]
            onw = _row(hg_o_norm_w[j], N_HEADS)
            w_o = hg_w_o[j].astype(BF16)
        else:
            p = {"w_qkv": df_w_qkv[j].astype(BF16), "qn_w": _row(df_qn_w[j], d // ROPE_DIM),
                 "kn_w": _row(df_kn_w[j], d // ROPE_DIM)}
            q, k, v = _df_pre_call(h, mods, i, n_lat_tiles, n1, p, cos_t, sin_t)
            lam_init = 0.8 - 0.6 * math.exp(-0.3 * i)
            mixer_out = [_attend(q, k, v, seq, not last,
                                 extra=(df_lambda[j].astype(F32), _row(df_sub_norm_w[j])),
                                 lam_init=lam_init)]
            w_o = df_w_o[j].astype(BF16)
        h = _post_call(h, mods, i, n_lat_tiles, n_out_rows, mixer_out, w_o, n2, w13, w2, onw)
    return h
```

```python
import functools
import math

import jax
import jax.numpy as jnp
from jax import lax
from jax.experimental import pallas as pl
from jax.experimental.pallas import tpu as pltpu

F32 = jnp.float32
BF16 = jnp.bfloat16

EPS = 1e-6
ROPE_BASE = 10000.0
GRID_W = 64
N_MOD = 6
N_MIXERS = 3

N_HEADS = 8
LANES = 128
MLA_NOPE = 128
MLA_ROPE = 64
MLA_V = 128
MLA_Q_RANK = 384
MLA_KV_RANK = 256
MLA_DK = 256
ROPE_DIM = 64
ROPE_QUARTER = ROPE_DIM // 4

TOKEN_TILE = 256
ATTN_TQ = 512
ATTN_TKB = 2048
ATTN_DT = jnp.float8_e4m3fn
P_SHIFT = 8.0
LOG2_E = math.log2(math.e)
HG_CHUNK = 64
HG_SUB = 16
VMEM_LIMIT_BYTES = 56 * 1024 * 1024


def _cparams(*sem):
    return pltpu.CompilerParams(dimension_semantics=sem, vmem_limit_bytes=VMEM_LIMIT_BYTES)


def _const_spec(shape):
    nd = len(shape)
    return pl.BlockSpec(shape, lambda *_: (0,) * nd, pipeline_mode=pl.Buffered(1))


def _dot(a, b):
    return jnp.dot(a, b, preferred_element_type=F32)


def _dot_nt(a, b):
    return lax.dot_general(a, b, (((1,), (1,)), ((), ())), preferred_element_type=F32)


def _dot_tn(a, b):
    return lax.dot_general(a, b, (((0,), (0,)), ((), ())), preferred_element_type=F32)


def _silu(t):
    return t * jax.nn.sigmoid(t)


def _rms(t, w):
    return t * lax.rsqrt(jnp.mean(t * t, axis=-1, keepdims=True) + EPS) * w


def _norm_mod(t, w, shift, scale):
    return _rms(t, w) * (1.0 + scale) + shift


def _group_rms(t, w, group):
    width = t.shape[-1]
    lane = lax.broadcasted_iota(jnp.int32, (1, LANES), 1)
    low = lane < 64
    outs = []
    for p in range(width // LANES):
        ts = t[:, p * LANES:(p + 1) * LANES]
        sq = ts * ts
        if group == LANES:
            ms = jnp.sum(sq, axis=-1, keepdims=True) * (1.0 / LANES)
        else:
            s_lo = jnp.sum(jnp.where(low, sq, 0.0), axis=-1, keepdims=True)
            s_hi = jnp.sum(jnp.where(low, 0.0, sq), axis=-1, keepdims=True)
            ms = jnp.where(low, s_lo, s_hi) * (1.0 / 64)
        outs.append(ts * lax.rsqrt(ms + EPS))
    return jnp.concatenate(outs, axis=-1) * w


def _ones_column(rows):
    lane = lax.broadcasted_iota(jnp.int32, (rows, LANES), 1)
    return jnp.where(lane == 0, 1.0, 0.0).astype(ATTN_DT)


def _rope(t, cos, sin):
    width = t.shape[-1]
    lane = lax.broadcasted_iota(jnp.int32, (1, width), 1)
    up = pltpu.roll(t, width - ROPE_QUARTER, 1)
    dn = pltpu.roll(t, ROPE_QUARTER, 1)
    rot = jnp.where((lane % (2 * ROPE_QUARTER)) < ROPE_QUARTER, -up, dn)
    return t * cos + rot * sin


def _ada_kernel(c_ref, w_ref, b_ref, o_ref):
    s = _silu(c_ref[...]).astype(BF16)
    o_ref[0] = _dot(s, w_ref[0].astype(BF16)) + b_ref[0]


def _ada_call(cvec, ada_w, ada_b):
    depth, d, n = ada_w.shape
    tn = 768
    return pl.pallas_call(
        _ada_kernel,
        grid=(depth, n // tn),
        in_specs=[
            pl.BlockSpec((8, d), lambda l, j: (0, 0)),
            pl.BlockSpec((1, d, tn), lambda l, j: (l, 0, j)),
            pl.BlockSpec((1, 1, tn), lambda l, j: (l, 0, j)),
        ],
        out_specs=pl.BlockSpec((1, 8, tn), lambda l, j: (l, 0, j)),
        out_shape=jax.ShapeDtypeStruct((depth, 8, n), F32),
        compiler_params=_cparams("parallel", "parallel"),
        name="ada_mod",
    )(cvec, ada_w, ada_b.reshape(depth, 1, n))


def _tok_spec(width, tm=TOKEN_TILE):
    return pl.BlockSpec((1, tm, width), lambda b, i: (b, i, 0))


def _mod_spec(layer, n_lat_tiles, ctx_row, d):
    return pl.BlockSpec((1, 1, N_MOD, d),
                        lambda b, i: (layer, jnp.where(i < n_lat_tiles, b, ctx_row), 0, 0))


def _head_spec(width, tm=TOKEN_TILE):
    return pl.BlockSpec((1, N_HEADS, tm, width), lambda b, i: (b, 0, i, 0))


def _mla_pre_kernel(x_ref, mod_ref, n1_ref, win_ref, qnw_ref, kvnw_ref, wuqn_ref, wuqr_ref,
                    wk_ref, wv_ref, qn_ref, qr_ref, kn_ref, kr_ref, cos_ref, sin_ref,
                    q_ref, k_ref, v_ref, *, scale):
    m = mod_ref[0, 0]
    a = _norm_mod(x_ref[0], n1_ref[...], m[0:1], m[1:2]).astype(BF16)
    t = _dot(a, win_ref[...])
    cq = _rms(t[:, :MLA_Q_RANK], qnw_ref[...]).astype(BF16)
    ckv = _rms(t[:, MLA_Q_RANK:MLA_Q_RANK + MLA_KV_RANK], kvnw_ref[...]).astype(BF16)
    kr = t[:, MLA_Q_RANK + MLA_KV_RANK:]
    cos = cos_ref[...]
    sin = sin_ref[...]
    kr = kr * lax.rsqrt(jnp.sum(kr * kr, axis=-1, keepdims=True) * (1.0 / MLA_ROPE) + EPS) * kr_ref[...]
    kr = _rope(kr, cos[:, :LANES], sin[:, :LANES]).astype(ATTN_DT)
    qn = _group_rms(_dot(cq, wuqn_ref[...]), qn_ref[...], LANES) * scale
    qr = _group_rms(_dot(cq, wuqr_ref[...]), qr_ref[...], 64)
    qr = _rope(qr, cos, sin) * scale
    kn = _group_rms(_dot(ckv, wk_ref[...]), kn_ref[...], LANES)
    v = _dot(ckv, wv_ref[...])
    for h in range(N_HEADS):
        sl = slice(h * LANES, (h + 1) * LANES)
        q_ref[0, h, :, :LANES] = qn[:, sl].astype(ATTN_DT)
        q_ref[0, h, :, LANES:] = qr[:, sl].astype(ATTN_DT)
        k_ref[0, h, :, :LANES] = kn[:, sl].astype(ATTN_DT)
        k_ref[0, h, :, LANES:] = kr
        v_ref[0, h, :, :LANES] = v[:, sl].astype(ATTN_DT)
        v_ref[0, h, :, LANES:] = _ones_column(v.shape[0])


def _mla_pre_call(h, mods, layer, n_lat_tiles, n1, p, cos_t, sin_t):
    b, nt, d = h.shape
    tm = TOKEN_TILE
    width = N_HEADS * LANES
    weights = [n1, p["w_in"], p["q_norm_w"], p["kv_norm_w"], p["w_uq_n"], p["w_uq_r"], p["w_uk"],
               p["w_uv"], p["qn_w"], p["qr_w"], p["kn_w"], p["kr_w"]]
    scale = LOG2_E / math.sqrt(MLA_NOPE + MLA_ROPE)
    return pl.pallas_call(
        functools.partial(_mla_pre_kernel, scale=scale),
        grid=(b, nt // tm),
        in_specs=[_tok_spec(d), _mod_spec(layer, n_lat_tiles, b, d)]
        + [_const_spec(w.shape) for w in weights]
        + [pl.BlockSpec((tm, width), lambda bb, i: (i, 0))] * 2,
        out_specs=[_head_spec(MLA_DK), _head_spec(MLA_DK), _head_spec(MLA_V + LANES)],
        out_shape=[jax.ShapeDtypeStruct((b, N_HEADS, nt, MLA_DK), ATTN_DT),
                   jax.ShapeDtypeStruct((b, N_HEADS, nt, MLA_DK), ATTN_DT),
                   jax.ShapeDtypeStruct((b, N_HEADS, nt, MLA_V + LANES), ATTN_DT)],
        compiler_params=_cparams("parallel", "parallel"),
        name="mla_pre",
    )(h, mods, *weights, cos_t, sin_t)


def _df_pre_kernel(x_ref, mod_ref, n1_ref, wqkv_ref, qn_ref, kn_ref, cos_ref, sin_ref,
                   q_ref, k_ref, v_ref, *, scale):
    m = mod_ref[0, 0]
    a = _norm_mod(x_ref[0], n1_ref[...], m[0:1], m[1:2]).astype(BF16)
    width = N_HEADS * LANES
    t = _dot(a, wqkv_ref[...])
    cos = cos_ref[...]
    sin = sin_ref[...]
    q = _rope(_group_rms(t[:, :width], qn_ref[...], 64), cos, sin) * scale
    k = _rope(_group_rms(t[:, width:2 * width], kn_ref[...], 64), cos, sin)
    v = t[:, 2 * width:]
    lane = lax.broadcasted_iota(jnp.int32, (1, LANES), 1)
    low = lane < 64
    for h in range(N_HEADS):
        sl = slice(h * LANES, (h + 1) * LANES)
        qh = q[:, sl]
        q_ref[0, h, 0] = jnp.where(low, qh, 0.0).astype(ATTN_DT)
        q_ref[0, h, 1] = jnp.where(low, 0.0, qh).astype(ATTN_DT)
        k_ref[0, h] = k[:, sl].astype(ATTN_DT)
        v_ref[0, h, :, :LANES] = v[:, sl].astype(ATTN_DT)
        v_ref[0, h, :, LANES:] = _ones_column(v.shape[0])


def _df_pre_call(h, mods, layer, n_lat_tiles, n1, p, cos_t, sin_t):
    b, nt, d = h.shape
    tm = TOKEN_TILE
    width = N_HEADS * LANES
    weights = [n1, p["w_qkv"], p["qn_w"], p["kn_w"]]
    return pl.pallas_call(
        functools.partial(_df_pre_kernel, scale=LOG2_E / math.sqrt(ROPE_DIM)),
        grid=(b, nt // tm),
        in_specs=[_tok_spec(d), _mod_spec(layer, n_lat_tiles, b, d)]
        + [_const_spec(w.shape) for w in weights]
        + [pl.BlockSpec((tm, width), lambda bb, i: (i, 0))] * 2,
        out_specs=[pl.BlockSpec((1, N_HEADS, 2, tm, LANES), lambda bb, i: (bb, 0, 0, i, 0)),
                   _head_spec(LANES), _head_spec(2 * LANES)],
        out_shape=[jax.ShapeDtypeStruct((b, N_HEADS, 2, nt, LANES), ATTN_DT),
                   jax.ShapeDtypeStruct((b, N_HEADS, nt, LANES), ATTN_DT),
                   jax.ShapeDtypeStruct((b, N_HEADS, nt, 2 * LANES), ATTN_DT)],
        compiler_params=_cparams("parallel", "parallel"),
        name="df_pre",
    )(h, mods, *weights, cos_t, sin_t)


def _flash_kernel(*refs, n_sub, head_blocks, n_loop, tkb, lam_init):
    if n_sub == 2:
        q_ref, k_ref, v_ref, lam_ref, snw_ref, o_ref = refs
    else:
        q_ref, k_ref, v_ref, o_ref = refs
    qs = [q_ref[0, 0, s] for s in range(n_sub)]
    tq = qs[0].shape[0]
    dv = v_ref.shape[-1] - LANES

    def step(carry, kb, vb):
        new = []
        for s in range(n_sub):
            m_old, acc = carry[s]
            sc = _dot_nt(qs[s], kb)
            m_new = jnp.maximum(m_old, jnp.max(sc, axis=-1, keepdims=True))
            p = jnp.exp2((sc - (m_new - P_SHIFT)).astype(BF16)).astype(ATTN_DT)
            alpha = jnp.exp2(m_old - m_new)
            new.append((m_new, alpha * acc + _dot(p, vb)))
        return tuple(new)

    carry = tuple((jnp.full((tq, 1), -jnp.inf, F32), jnp.zeros((tq, dv + LANES), F32))
                  for _ in range(n_sub))
    for start, rows in head_blocks:
        carry = step(carry, k_ref[0, 0, start:start + rows, :], v_ref[0, 0, start:start + rows, :])
    if n_loop:
        def body(j, c):
            off = pl.multiple_of(j * tkb, tkb)
            return step(c, k_ref[0, 0, pl.ds(off, tkb), :], v_ref[0, 0, pl.ds(off, tkb), :])
        carry = lax.fori_loop(0, n_loop, body, carry, unroll=True)
    outs = [acc[:, :dv] / acc[:, dv:dv + 1] for _, acc in carry]
    if n_sub == 1:
        o_ref[0] = outs[0].astype(o_ref.dtype)
    else:
        lv = lam_ref[...]
        lam = (jnp.exp(jnp.sum(lv[0:1] * lv[1:2], axis=-1, keepdims=True))
               - jnp.exp(jnp.sum(lv[2:3] * lv[3:4], axis=-1, keepdims=True)) + lam_init)
        o = outs[0] - lam * outs[1]
        o = _rms(o, snw_ref[...]) * (1.0 - lam_init)
        o_ref[0] = o.astype(o_ref.dtype)


def _flash_call(q, k, v, seq, ctx_queries, extra=(), lam_init=0.0):
    b, nh, n_sub, nt, dk = q.shape
    dve = v.shape[-1]
    dv = dve - LANES
    ctx_len = nt - seq
    if ctx_queries:
        tq, n_rows, q0 = ctx_len, ctx_len, seq // ctx_len
        kv_rows, kv0 = ctx_len, seq // ctx_len
        head_blocks, n_loop, tkb = ((0, ctx_len),), 0, 0
    else:
        tq, n_rows, q0 = min(ATTN_TQ, seq), seq, 0
        kv_rows, kv0 = nt, 0
        tkb = min(ATTN_TKB, seq)
        head_blocks, n_loop = ((seq, ctx_len),), seq // tkb
    in_specs = [pl.BlockSpec((1, 1, n_sub, tq, dk), lambda bb, hh, i: (bb, hh, 0, q0 + i, 0)),
                pl.BlockSpec((1, 1, kv_rows, dk), lambda bb, hh, i: (bb, hh, kv0, 0)),
                pl.BlockSpec((1, 1, kv_rows, dve), lambda bb, hh, i: (bb, hh, kv0, 0))]
    in_specs += [_const_spec(e.shape) for e in extra]
    return pl.pallas_call(
        functools.partial(_flash_kernel, n_sub=n_sub, head_blocks=head_blocks, n_loop=n_loop, tkb=tkb,
                          lam_init=lam_init),
        grid=(b, nh, n_rows // tq),
        in_specs=in_specs,
        out_specs=pl.BlockSpec((1, tq, dv), lambda bb, hh, i: (bb, i, hh)),
        out_shape=jax.ShapeDtypeStruct((b, n_rows, nh * dv), BF16),
        compiler_params=_cparams("parallel", "parallel", "arbitrary"),
        name="flash_ctx" if ctx_queries else "flash_lat",
    )(q, k, v, *extra)


def _attend(q, k, v, seq, with_ctx, extra=(), lam_init=0.0):
    o = _flash_call(q, k, v, seq, False, extra, lam_init)
    if with_ctx:
        o = jnp.concatenate([o, _flash_call(q, k, v, seq, True, extra, lam_init)], axis=1)
    return o


def _hg_pre_kernel(x_ref, mod_ref, n1_ref, win_ref, lbl_ref, q_ref, i_ref, g_ref, lf_ref, *, layer):
    m = mod_ref[0, 0]
    a = _norm_mod(x_ref[0], n1_ref[...], m[0:1], m[1:2]).astype(BF16)
    width = N_HEADS * LANES
    t = _dot(a, win_ref[...])
    logits = lbl_ref[...]
    e = jnp.exp(logits - jnp.max(logits, axis=0, keepdims=True))
    sm = e / jnp.sum(e, axis=0, keepdims=True)
    lb = jnp.zeros_like(sm[0])
    for j in range(1, layer + 1):
        lb = lb + sm[j]
    q_ref[0] = t[:, :width]
    i_ref[0] = t[:, 3 * width:4 * width]
    g_ref[0] = t[:, 4 * width:]
    for dirn in range(2):
        z = t[:, (1 + dirn) * width:(2 + dirn) * width]
        lbd = lb[dirn:dirn + 1]
        lf_ref[0, dirn] = jnp.log(lbd + (1.0 - lbd) * jax.nn.sigmoid(z))


def _hg_pre_call(h, mods, layer, n_lat_tiles, n1, w_in, lb_logits):
    b, nt, d = h.shape
    tm = TOKEN_TILE
    weights = [n1, w_in, lb_logits]
    tok = jax.ShapeDtypeStruct((b, nt, d), F32)
    return pl.pallas_call(
        functools.partial(_hg_pre_kernel, layer=layer),
        grid=(b, nt // tm),
        in_specs=[_tok_spec(d), _mod_spec(layer, n_lat_tiles, b, d)] + [_const_spec(w.shape) for w in weights],
        out_specs=[_tok_spec(d), _tok_spec(d), _tok_spec(d),
                   pl.BlockSpec((1, 2, tm, d), lambda bb, i: (bb, 0, i, 0))],
        out_shape=[tok, tok, tok, jax.ShapeDtypeStruct((b, 2, nt, d), F32)],
        compiler_params=_cparams("parallel", "parallel"),
        name="hg_pre",
    )(h, mods, *weights)


def _hg_scan_kernel(q_ref, i_ref, lf_ref, o_ref, st_ref, *, rev):
    @pl.when(pl.program_id(1) == 0)
    def _():
        st_ref[...] = jnp.zeros_like(st_ref)

    q = q_ref[0]
    v = i_ref[0]
    lf = lf_ref[0, 0]
    c = q.shape[0]
    row = lax.broadcasted_iota(jnp.int32, (c, c), 0)
    col = lax.broadcasted_iota(jnp.int32, (c, c), 1)
    tri = jnp.where((col >= row) if rev else (col <= row), 1.0, 0.0).astype(BF16)
    hi = lf.astype(BF16)
    r1 = lf - hi.astype(F32)
    mid = r1.astype(BF16)
    lo = (r1 - mid.astype(F32)).astype(BF16)
    cum = _dot(tri, hi) + _dot(tri, mid) + _dot(tri, lo)
    key = 1.0 - jnp.exp(lf)
    tot = cum[0:1] if rev else cum[c - 1:c]
    q_dec = (q * jnp.exp(cum)).astype(BF16)
    k_dec = (key * jnp.exp(tot - cum)).astype(BF16)
    v16 = v.astype(BF16)
    rows = lax.broadcasted_iota(jnp.int32, (c, 1), 0)
    cols = lax.broadcasted_iota(jnp.int32, (1, HG_SUB), 1)
    for h in range(N_HEADS):
        sl = slice(h * LANES, (h + 1) * LANES)
        st = st_ref[h]
        o = _dot_nt(q_dec[:, sl], st.astype(BF16))
        for jb in range(c // HG_SUB):
            r0 = jb * HG_SUB
            mid_row = r0 + HG_SUB // 2
            ref = cum[mid_row:mid_row + 1, sl]
            kt = (key[r0:r0 + HG_SUB, sl] * jnp.exp(ref - cum[r0:r0 + HG_SUB, sl])).astype(BF16)
            used = (rows < r0 + HG_SUB) if rev else (rows >= r0)
            qt = (q[:, sl] * jnp.exp(jnp.where(used, cum[:, sl] - ref, 0.0))).astype(BF16)
            att = _dot_nt(qt, kt)
            keep = (rows <= r0 + cols) if rev else (rows >= r0 + cols)
            att = jnp.where(keep, att, 0.0).astype(BF16)
            o = o + _dot(att, v16[r0:r0 + HG_SUB, sl])
        o_ref[0, :, sl] = o
        st_ref[h] = st * jnp.exp(tot[:, sl]) + _dot_tn(v16[:, sl], k_dec[:, sl])


def _hg_scan_call(q, i, lf, seq, rev):
    b, nt, d = q.shape
    c = HG_CHUNK
    n_lat = seq // c
    n_ctx = (nt - seq) // c
    dirn = 1 if rev else 0

    def blk(j):
        if rev:
            return n_lat + n_ctx - 1 - j
        return jnp.where(j < n_ctx, n_lat + j, j - n_ctx)

    return pl.pallas_call(
        functools.partial(_hg_scan_kernel, rev=rev),
        grid=(b, nt // c),
        in_specs=[pl.BlockSpec((1, c, d), lambda bb, j: (bb, blk(j), 0)),
                  pl.BlockSpec((1, c, d), lambda bb, j: (bb, blk(j), 0)),
                  pl.BlockSpec((1, 1, c, d), lambda bb, j: (bb, dirn, blk(j), 0))],
        out_specs=pl.BlockSpec((1, c, d), lambda bb, j: (bb, blk(j), 0)),
        out_shape=jax.ShapeDtypeStruct((b, nt, d), F32),
        scratch_shapes=[pltpu.VMEM((N_HEADS, LANES, LANES), F32)],
        compiler_params=_cparams("parallel", "arbitrary"),
        name="hg_scan_rev" if rev else "hg_scan_fwd",
    )(q, i, lf)


def _post_kernel(*refs, mode, d_ff):
    if mode == "hg":
        x_ref, mod_ref, of_ref, ob_ref, g_ref, onw_ref, wo_ref, n2_ref, w13_ref, w2_ref, out_ref = refs
        o = _group_rms(of_ref[0] + ob_ref[0], onw_ref[...], LANES) * _silu(g_ref[0])
        o = o.astype(BF16)
    else:
        x_ref, mod_ref, o_ref, wo_ref, n2_ref, w13_ref, w2_ref, out_ref = refs
        o = o_ref[0]
    m = mod_ref[0, 0]
    x1 = x_ref[0] + m[2:3] * _dot(o, wo_ref[...])
    a2 = _norm_mod(x1, n2_ref[...], m[3:4], m[4:5]).astype(BF16)
    hc = _dot(a2, w13_ref[...])
    hid = (_silu(hc[:, :d_ff]) * hc[:, d_ff:]).astype(BF16)
    out_ref[0] = x1 + m[5:6] * _dot(hid, w2_ref[...])


def _post_call(h, mods, layer, n_lat_tiles, n_out_rows, mixer_out, w_o, n2, w13, w2, onw=None):
    b, nt, d = h.shape
    tm = TOKEN_TILE
    d_ff = w2.shape[0]
    mode = "hg" if onw is not None else "plain"
    acts = list(mixer_out)
    weights = ([onw] if mode == "hg" else []) + [w_o, n2, w13, w2]
    return pl.pallas_call(
        functools.partial(_post_kernel, mode=mode, d_ff=d_ff),
        grid=(b, n_out_rows // tm),
        in_specs=[_tok_spec(d), _mod_spec(layer, n_lat_tiles, b, d)]
        + [_tok_spec(d) for _ in acts] + [_const_spec(w.shape) for w in weights],
        out_specs=_tok_spec(d),
        out_shape=jax.ShapeDtypeStruct((b, n_out_rows, d), F32),
        compiler_params=_cparams("parallel", "parallel"),
        name="post_" + mode,
    )(h, mods, *acts, *weights)


def _rope_tables(seq, ctx_len, reps):
    rows = seq // GRID_W
    row = jnp.repeat(jnp.arange(rows, dtype=F32), GRID_W)
    col = jnp.tile(jnp.arange(GRID_W, dtype=F32), rows)
    axis_dim = ROPE_DIM // 2
    inv_freq = jnp.power(ROPE_BASE, -jnp.arange(0, axis_dim, 2, dtype=F32) / axis_dim)
    ang_r = row[:, None] * inv_freq
    ang_c = col[:, None] * inv_freq
    ang = jnp.concatenate([ang_r, ang_r, ang_c, ang_c], axis=-1)
    cos = jnp.concatenate([jnp.cos(ang), jnp.ones((ctx_len, ROPE_DIM), F32)], axis=0)
    sin = jnp.concatenate([jnp.sin(ang), jnp.zeros((ctx_len, ROPE_DIM), F32)], axis=0)
    return jnp.tile(cos, (1, reps)), jnp.tile(sin, (1, reps))


def _row(w, reps=1):
    return jnp.tile(w.astype(F32), reps)[None, :]


def _pad_groups(w, group, n_groups, to):
    lead = w.shape[:-1]
    w = w.reshape(lead + (n_groups, group))
    w = jnp.pad(w, [(0, 0)] * len(lead) + [(0, 0), (0, to - group)])
    return w.reshape(lead + (n_groups * to,))


def _mla_params(j, mla_w_in, mla_q_norm_w, mla_kv_norm_w, mla_w_uq, mla_w_ukv, mla_qn_w, mla_qr_w,
                mla_kn_w, mla_kr_w):
    hq = MLA_NOPE + MLA_ROPE
    w_uq = mla_w_uq[j].reshape(MLA_Q_RANK, N_HEADS, hq)
    w_ukv = mla_w_ukv[j].reshape(MLA_KV_RANK, N_HEADS, MLA_NOPE + MLA_V)
    return {
        "w_in": jnp.pad(mla_w_in[j], ((0, 0), (0, LANES - MLA_ROPE))).astype(BF16),
        "q_norm_w": _row(mla_q_norm_w[j]),
        "kv_norm_w": _row(mla_kv_norm_w[j]),
        "w_uq_n": w_uq[:, :, :MLA_NOPE].reshape(MLA_Q_RANK, -1).astype(BF16),
        "w_uq_r": _pad_groups(w_uq[:, :, MLA_NOPE:].reshape(MLA_Q_RANK, -1), MLA_ROPE, N_HEADS,
                              LANES).astype(BF16),
        "w_uk": w_ukv[:, :, :MLA_NOPE].reshape(MLA_KV_RANK, -1).astype(BF16),
        "w_uv": w_ukv[:, :, MLA_NOPE:].reshape(MLA_KV_RANK, -1).astype(BF16),
        "qn_w": _row(mla_qn_w[j], N_HEADS),
        "qr_w": _row(_pad_groups(mla_qr_w[j], MLA_ROPE, 1, LANES), N_HEADS),
        "kn_w": _row(mla_kn_w[j], N_HEADS),
        "kr_w": _row(_pad_groups(mla_kr_w[j], MLA_ROPE, 1, LANES)),
    }


def kernel(x, c, ctx, c_ctx, ada_w, ada_b, norm1_w, norm2_w, ffn_w1, ffn_w3, ffn_w2, mla_w_in, mla_q_norm_w, mla_kv_norm_w, mla_w_uq, mla_w_ukv, mla_qn_w, mla_qr_w, mla_kn_w, mla_kr_w, mla_w_o, hg_w_in, hg_lb_logits, hg_o_norm_w, hg_w_o, df_w_qkv, df_qn_w, df_kn_w, df_lambda, df_sub_norm_w, df_w_o):
    b, seq, d = x.shape
    ctx_len = ctx.shape[1]
    depth = ada_w.shape[0]
    tm = TOKEN_TILE
    assert d == N_HEADS * LANES and b < 8
    assert seq % tm == 0 and ctx_len % tm == 0 and seq % GRID_W == 0
    assert seq % ctx_len == 0 and seq % min(ATTN_TQ, seq) == 0 and seq % min(ATTN_TKB, seq) == 0
    assert ctx_len % HG_CHUNK == 0 and seq % HG_CHUNK == 0
    nt = seq + ctx_len
    n_lat_tiles = seq // tm

    h = jnp.concatenate([x, ctx], axis=1)
    cvec = jnp.zeros((8, d), F32).at[:b].set(c).at[b].set(c_ctx)
    mods = _ada_call(cvec, ada_w, ada_b).reshape(depth, 8, N_MOD, d)
    cos_t, sin_t = _rope_tables(seq, ctx_len, d // ROPE_DIM)

    for i in range(depth):
        kind, j = i % N_MIXERS, i // N_MIXERS
        last = i == depth - 1
        n_out_rows = seq if last else nt
        n1 = _row(norm1_w[i])
        n2 = _row(norm2_w[i])
        w13 = jnp.concatenate([ffn_w1[i], ffn_w3[i]], axis=1).astype(BF16)
        w2 = ffn_w2[i].astype(BF16)
        onw = None
        if kind == 0:
            p = _mla_params(j, mla_w_in, mla_q_norm_w, mla_kv_norm_w, mla_w_uq, mla_w_ukv, mla_qn_w,
                            mla_qr_w, mla_kn_w, mla_kr_w)
            q, k, v = _mla_pre_call(h, mods, i, n_lat_tiles, n1, p, cos_t, sin_t)
            mixer_out = [_attend(q[:, :, None], k, v, seq, not last)]
            w_o = mla_w_o[j].astype(BF16)
        elif kind == 1:
            q, gi, g, lf = _hg_pre_call(h, mods, i, n_lat_tiles, n1, hg_w_in[j].astype(BF16),
                                        hg_lb_logits.astype(F32))
            mixer_out = [_hg_scan_call(q, gi, lf, seq, False), _hg_scan_call(q, gi, lf, seq, True), g]
            onw = _row(hg_o_norm_w[j], N_HEADS)
            w_o = hg_w_o[j].astype(BF16)
        else:
            p = {"w_qkv": df_w_qkv[j].astype(BF16), "qn_w": _row(df_qn_w[j], d // ROPE_DIM),
                 "kn_w": _row(df_kn_w[j], d // ROPE_DIM)}
            q, k, v = _df_pre_call(h, mods, i, n_lat_tiles, n1, p, cos_t, sin_t)
            lam_init = 0.8 - 0.6 * math.exp(-0.3 * i)
            mixer_out = [_attend(q, k, v, seq, not last,
                                 extra=(df_lambda[j].astype(F32), _row(df_sub_norm_w[j])),
                                 lam_init=lam_init)]
            w_o = df_w_o[j].astype(BF16)
        h = _post_call(h, mods, i, n_lat_tiles, n_out_rows, mixer_out, w_o, n2, w13, w2, onw)
    return h
```

```python
import functools
import math

import jax
import jax.numpy as jnp
from jax import lax
from jax.experimental import pallas as pl
from jax.experimental.pallas import tpu as pltpu

F32 = jnp.float32
BF16 = jnp.bfloat16

EPS = 1e-6
ROPE_BASE = 10000.0
GRID_W = 64
N_MOD = 6
N_MIXERS = 3

N_HEADS = 8
LANES = 128
MLA_NOPE = 128
MLA_ROPE = 64
MLA_V = 128
MLA_Q_RANK = 384
MLA_KV_RANK = 256
MLA_DK = 256
ROPE_DIM = 64
ROPE_QUARTER = ROPE_DIM // 4

TOKEN_TILE = 256
ATTN_TQ = 512
ATTN_TKB = 2048
ATTN_DT = jnp.float8_e4m3fn
P_SHIFT = 8.0
LOG2_E = math.log2(math.e)
HG_CHUNK = 128
HG_SUB = 16
VMEM_LIMIT_BYTES = 56 * 1024 * 1024


def _cparams(*sem):
    return pltpu.CompilerParams(dimension_semantics=sem, vmem_limit_bytes=VMEM_LIMIT_BYTES)


def _const_spec(shape):
    nd = len(shape)
    return pl.BlockSpec(shape, lambda *_: (0,) * nd, pipeline_mode=pl.Buffered(1))


def _dot(a, b):
    return jnp.dot(a, b, preferred_element_type=F32)


def _dot_nt(a, b):
    return lax.dot_general(a, b, (((1,), (1,)), ((), ())), preferred_element_type=F32)


def _dot_tn(a, b):
    return lax.dot_general(a, b, (((0,), (0,)), ((), ())), preferred_element_type=F32)


def _silu(t):
    return t * jax.nn.sigmoid(t)


def _rms(t, w):
    return t * lax.rsqrt(jnp.mean(t * t, axis=-1, keepdims=True) + EPS) * w


def _norm_mod(t, w, shift, scale):
    return _rms(t, w) * (1.0 + scale) + shift


def _group_rms(t, w, group):
    width = t.shape[-1]
    lane = lax.broadcasted_iota(jnp.int32, (1, LANES), 1)
    low = lane < 64
    outs = []
    for p in range(width // LANES):
        ts = t[:, p * LANES:(p + 1) * LANES]
        sq = ts * ts
        if group == LANES:
            ms = jnp.sum(sq, axis=-1, keepdims=True) * (1.0 / LANES)
        else:
            s_lo = jnp.sum(jnp.where(low, sq, 0.0), axis=-1, keepdims=True)
            s_hi = jnp.sum(jnp.where(low, 0.0, sq), axis=-1, keepdims=True)
            ms = jnp.where(low, s_lo, s_hi) * (1.0 / 64)
        outs.append(ts * lax.rsqrt(ms + EPS))
    return jnp.concatenate(outs, axis=-1) * w


def _ones_column(rows):
    lane = lax.broadcasted_iota(jnp.int32, (rows, LANES), 1)
    return jnp.where(lane == 0, 1.0, 0.0).astype(ATTN_DT)


def _rope(t, cos, sin):
    width = t.shape[-1]
    lane = lax.broadcasted_iota(jnp.int32, (1, width), 1)
    up = pltpu.roll(t, width - ROPE_QUARTER, 1)
    dn = pltpu.roll(t, ROPE_QUARTER, 1)
    rot = jnp.where((lane % (2 * ROPE_QUARTER)) < ROPE_QUARTER, -up, dn)
    return t * cos + rot * sin


def _ada_kernel(c_ref, w_ref, b_ref, o_ref):
    s = _silu(c_ref[...]).astype(BF16)
    o_ref[0] = _dot(s, w_ref[0].astype(BF16)) + b_ref[0]


def _ada_call(cvec, ada_w, ada_b):
    depth, d, n = ada_w.shape
    tn = 768
    return pl.pallas_call(
        _ada_kernel,
        grid=(depth, n // tn),
        in_specs=[
            pl.BlockSpec((8, d), lambda l, j: (0, 0)),
            pl.BlockSpec((1, d, tn), lambda l, j: (l, 0, j)),
            pl.BlockSpec((1, 1, tn), lambda l, j: (l, 0, j)),
        ],
        out_specs=pl.BlockSpec((1, 8, tn), lambda l, j: (l, 0, j)),
        out_shape=jax.ShapeDtypeStruct((depth, 8, n), F32),
        compiler_params=_cparams("parallel", "parallel"),
        name="ada_mod",
    )(cvec, ada_w, ada_b.reshape(depth, 1, n))


def _tok_spec(width, tm=TOKEN_TILE):
    return pl.BlockSpec((1, tm, width), lambda b, i: (b, i, 0))


def _mod_spec(layer, n_lat_tiles, ctx_row, d):
    return pl.BlockSpec((1, 1, N_MOD, d),
                        lambda b, i: (layer, jnp.where(i < n_lat_tiles, b, ctx_row), 0, 0))


def _head_spec(width, tm=TOKEN_TILE):
    return pl.BlockSpec((1, N_HEADS, tm, width), lambda b, i: (b, 0, i, 0))


def _mla_pre_kernel(x_ref, mod_ref, n1_ref, win_ref, qnw_ref, kvnw_ref, wuqn_ref, wuqr_ref,
                    wk_ref, wv_ref, qn_ref, qr_ref, kn_ref, kr_ref, cos_ref, sin_ref,
                    q_ref, k_ref, v_ref, *, scale):
    m = mod_ref[0, 0]
    a = _norm_mod(x_ref[0], n1_ref[...], m[0:1], m[1:2]).astype(BF16)
    t = _dot(a, win_ref[...])
    cq = _rms(t[:, :MLA_Q_RANK], qnw_ref[...]).astype(BF16)
    ckv = _rms(t[:, MLA_Q_RANK:MLA_Q_RANK + MLA_KV_RANK], kvnw_ref[...]).astype(BF16)
    kr = t[:, MLA_Q_RANK + MLA_KV_RANK:]
    cos1 = cos_ref[...]
    sin1 = sin_ref[...]
    cos = jnp.concatenate([cos1] * N_HEADS, axis=-1)
    sin = jnp.concatenate([sin1] * N_HEADS, axis=-1)
    kr = kr * lax.rsqrt(jnp.sum(kr * kr, axis=-1, keepdims=True) * (1.0 / MLA_ROPE) + EPS) * kr_ref[...]
    kr = _rope(kr, cos1, sin1).astype(ATTN_DT)
    qn = _group_rms(_dot(cq, wuqn_ref[...]), qn_ref[...], LANES) * scale
    qr = _group_rms(_dot(cq, wuqr_ref[...]), qr_ref[...], 64)
    qr = _rope(qr, cos, sin) * scale
    kn = _group_rms(_dot(ckv, wk_ref[...]), kn_ref[...], LANES)
    v = _dot(ckv, wv_ref[...])
    for h in range(N_HEADS):
        sl = slice(h * LANES, (h + 1) * LANES)
        q_ref[0, h, :, :LANES] = qn[:, sl].astype(ATTN_DT)
        q_ref[0, h, :, LANES:] = qr[:, sl].astype(ATTN_DT)
        k_ref[0, h, :, :LANES] = kn[:, sl].astype(ATTN_DT)
        k_ref[0, h, :, LANES:] = kr
        v_ref[0, h, :, :LANES] = v[:, sl].astype(ATTN_DT)
        v_ref[0, h, :, LANES:] = _ones_column(v.shape[0])


def _mla_pre_call(h, mods, layer, n_lat_tiles, n1, p, cos_t, sin_t):
    b, nt, d = h.shape
    tm = TOKEN_TILE
    width = N_HEADS * LANES
    weights = [n1, p["w_in"], p["q_norm_w"], p["kv_norm_w"], p["w_uq_n"], p["w_uq_r"], p["w_uk"],
               p["w_uv"], p["qn_w"], p["qr_w"], p["kn_w"], p["kr_w"]]
    scale = LOG2_E / math.sqrt(MLA_NOPE + MLA_ROPE)
    return pl.pallas_call(
        functools.partial(_mla_pre_kernel, scale=scale),
        grid=(b, nt // tm),
        in_specs=[_tok_spec(d), _mod_spec(layer, n_lat_tiles, b, d)]
        + [_const_spec(w.shape) for w in weights]
        + [pl.BlockSpec((tm, LANES), lambda bb, i: (i, 0))] * 2,
        out_specs=[_head_spec(MLA_DK), _head_spec(MLA_DK), _head_spec(MLA_V + LANES)],
        out_shape=[jax.ShapeDtypeStruct((b, N_HEADS, nt, MLA_DK), ATTN_DT),
                   jax.ShapeDtypeStruct((b, N_HEADS, nt, MLA_DK), ATTN_DT),
                   jax.ShapeDtypeStruct((b, N_HEADS, nt, MLA_V + LANES), ATTN_DT)],
        compiler_params=_cparams("parallel", "parallel"),
        name="mla_pre",
    )(h, mods, *weights, cos_t, sin_t)


def _df_pre_kernel(x_ref, mod_ref, n1_ref, wqkv_ref, qn_ref, kn_ref, cos_ref, sin_ref,
                   q_ref, k_ref, v_ref, *, scale):
    m = mod_ref[0, 0]
    a = _norm_mod(x_ref[0], n1_ref[...], m[0:1], m[1:2]).astype(BF16)
    width = N_HEADS * LANES
    t = _dot(a, wqkv_ref[...])
    cos = jnp.concatenate([cos_ref[...]] * N_HEADS, axis=-1)
    sin = jnp.concatenate([sin_ref[...]] * N_HEADS, axis=-1)
    q = _rope(_group_rms(t[:, :width], qn_ref[...], 64), cos, sin) * scale
    k = _rope(_group_rms(t[:, width:2 * width], kn_ref[...], 64), cos, sin)
    v = t[:, 2 * width:]
    lane = lax.broadcasted_iota(jnp.int32, (1, LANES), 1)
    low = lane < 64
    for h in range(N_HEADS):
        sl = slice(h * LANES, (h + 1) * LANES)
        qh = q[:, sl]
        q_ref[0, h, 0] = jnp.where(low, qh, 0.0).astype(ATTN_DT)
        q_ref[0, h, 1] = jnp.where(low, 0.0, qh).astype(ATTN_DT)
        k_ref[0, h] = k[:, sl].astype(ATTN_DT)
        v_ref[0, h, :, :LANES] = v[:, sl].astype(ATTN_DT)
        v_ref[0, h, :, LANES:] = _ones_column(v.shape[0])


def _df_pre_call(h, mods, layer, n_lat_tiles, n1, p, cos_t, sin_t):
    b, nt, d = h.shape
    tm = TOKEN_TILE
    width = N_HEADS * LANES
    weights = [n1, p["w_qkv"], p["qn_w"], p["kn_w"]]
    return pl.pallas_call(
        functools.partial(_df_pre_kernel, scale=LOG2_E / math.sqrt(ROPE_DIM)),
        grid=(b, nt // tm),
        in_specs=[_tok_spec(d), _mod_spec(layer, n_lat_tiles, b, d)]
        + [_const_spec(w.shape) for w in weights]
        + [pl.BlockSpec((tm, LANES), lambda bb, i: (i, 0))] * 2,
        out_specs=[pl.BlockSpec((1, N_HEADS, 2, tm, LANES), lambda bb, i: (bb, 0, 0, i, 0)),
                   _head_spec(LANES), _head_spec(2 * LANES)],
        out_shape=[jax.ShapeDtypeStruct((b, N_HEADS, 2, nt, LANES), ATTN_DT),
                   jax.ShapeDtypeStruct((b, N_HEADS, nt, LANES), ATTN_DT),
                   jax.ShapeDtypeStruct((b, N_HEADS, nt, 2 * LANES), ATTN_DT)],
        compiler_params=_cparams("parallel", "parallel"),
        name="df_pre",
    )(h, mods, *weights, cos_t, sin_t)


def _flash_kernel(*refs, n_sub, head_blocks, n_loop, tkb, lam_init):
    if n_sub == 2:
        q_ref, k_ref, v_ref, lam_ref, snw_ref, o_ref = refs
    else:
        q_ref, k_ref, v_ref, o_ref = refs
    qs = [q_ref[0, 0, s] for s in range(n_sub)]
    tq = qs[0].shape[0]
    dv = v_ref.shape[-1] - LANES

    def step(carry, kb, vb):
        new = []
        for s in range(n_sub):
            m_old, c_old, acc = carry[s]
            sc = _dot_nt(qs[s], kb).astype(BF16)
            m_new = jnp.maximum(m_old, jnp.max(sc, axis=-1, keepdims=True).astype(F32))
            c_new = (m_new - P_SHIFT).astype(BF16)
            p = jnp.exp2(sc - c_new).astype(ATTN_DT)
            alpha = jnp.exp2(c_old - c_new.astype(F32))
            new.append((m_new, c_new.astype(F32), alpha * acc + _dot(p, vb)))
        return tuple(new)

    carry = tuple((jnp.full((tq, 1), -jnp.inf, F32), jnp.full((tq, 1), -jnp.inf, F32),
                   jnp.zeros((tq, dv + LANES), F32)) for _ in range(n_sub))
    for start, rows in head_blocks:
        carry = step(carry, k_ref[0, 0, start:start + rows, :], v_ref[0, 0, start:start + rows, :])
    if n_loop:
        def body(j, c):
            off = pl.multiple_of(j * tkb, tkb)
            return step(c, k_ref[0, 0, pl.ds(off, tkb), :], v_ref[0, 0, pl.ds(off, tkb), :])
        carry = lax.fori_loop(0, n_loop, body, carry, unroll=True)
    outs = [acc[:, :dv] / acc[:, dv:dv + 1] for _, _, acc in carry]
    if n_sub == 1:
        o_ref[0] = outs[0].astype(o_ref.dtype)
    else:
        lv = lam_ref[...]
        lam = (jnp.exp(jnp.sum(lv[0:1] * lv[1:2], axis=-1, keepdims=True))
               - jnp.exp(jnp.sum(lv[2:3] * lv[3:4], axis=-1, keepdims=True)) + lam_init)
        o = outs[0] - lam * outs[1]
        o = _rms(o, snw_ref[...]) * (1.0 - lam_init)
        o_ref[0] = o.astype(o_ref.dtype)


def _flash_call(q, k, v, seq, ctx_queries, extra=(), lam_init=0.0):
    b, nh, n_sub, nt, dk = q.shape
    dve = v.shape[-1]
    dv = dve - LANES
    ctx_len = nt - seq
    if ctx_queries:
        tq, n_rows, q0 = ctx_len, ctx_len, seq // ctx_len
        kv_rows, kv0 = ctx_len, seq // ctx_len
        head_blocks, n_loop, tkb = ((0, ctx_len),), 0, 0
    else:
        tq, n_rows, q0 = min(ATTN_TQ, seq), seq, 0
        kv_rows, kv0 = nt, 0
        tkb = min(ATTN_TKB, seq)
        head_blocks, n_loop = ((seq, ctx_len),), seq // tkb
    in_specs = [pl.BlockSpec((1, 1, n_sub, tq, dk), lambda bb, hh, i: (bb, hh, 0, q0 + i, 0)),
                pl.BlockSpec((1, 1, kv_rows, dk), lambda bb, hh, i: (bb, hh, kv0, 0)),
                pl.BlockSpec((1, 1, kv_rows, dve), lambda bb, hh, i: (bb, hh, kv0, 0))]
    in_specs += [_const_spec(e.shape) for e in extra]
    return pl.pallas_call(
        functools.partial(_flash_kernel, n_sub=n_sub, head_blocks=head_blocks, n_loop=n_loop, tkb=tkb,
                          lam_init=lam_init),
        grid=(b, nh, n_rows // tq),
        in_specs=in_specs,
        out_specs=pl.BlockSpec((1, tq, dv), lambda bb, hh, i: (bb, i, hh)),
        out_shape=jax.ShapeDtypeStruct((b, n_rows, nh * dv), BF16),
        compiler_params=_cparams("parallel", "parallel", "arbitrary"),
        name="flash_ctx" if ctx_queries else "flash_lat",
    )(q, k, v, *extra)


def _attend(q, k, v, seq, with_ctx, extra=(), lam_init=0.0):
    o = _flash_call(q, k, v, seq, False, extra, lam_init)
    if with_ctx:
        o = jnp.concatenate([o, _flash_call(q, k, v, seq, True, extra, lam_init)], axis=1)
    return o


def _hg_pre_kernel(x_ref, mod_ref, n1_ref, win_ref, lbl_ref, q_ref, i_ref, g_ref, lf_ref, *, layer):
    m = mod_ref[0, 0]
    a = _norm_mod(x_ref[0], n1_ref[...], m[0:1], m[1:2]).astype(BF16)
    width = N_HEADS * LANES
    t = _dot(a, win_ref[...])
    logits = lbl_ref[...]
    e = jnp.exp(logits - jnp.max(logits, axis=0, keepdims=True))
    sm = e / jnp.sum(e, axis=0, keepdims=True)
    lb = jnp.zeros_like(sm[0])
    for j in range(1, layer + 1):
        lb = lb + sm[j]
    q_ref[0] = t[:, :width]
    i_ref[0] = t[:, 3 * width:4 * width]
    g_ref[0] = t[:, 4 * width:]
    for dirn in range(2):
        z = t[:, (1 + dirn) * width:(2 + dirn) * width]
        lbd = lb[dirn:dirn + 1]
        lf_ref[0, dirn] = jnp.log(lbd + (1.0 - lbd) * jax.nn.sigmoid(z))


def _hg_pre_call(h, mods, layer, n_lat_tiles, n1, w_in, lb_logits):
    b, nt, d = h.shape
    tm = TOKEN_TILE
    weights = [n1, w_in, lb_logits]
    tok = jax.ShapeDtypeStruct((b, nt, d), F32)
    return pl.pallas_call(
        functools.partial(_hg_pre_kernel, layer=layer),
        grid=(b, nt // tm),
        in_specs=[_tok_spec(d), _mod_spec(layer, n_lat_tiles, b, d)] + [_const_spec(w.shape) for w in weights],
        out_specs=[_tok_spec(d), _tok_spec(d), _tok_spec(d),
                   pl.BlockSpec((1, 2, tm, d), lambda bb, i: (bb, 0, i, 0))],
        out_shape=[tok, tok, tok, jax.ShapeDtypeStruct((b, 2, nt, d), F32)],
        compiler_params=_cparams("parallel", "parallel"),
        name="hg_pre",
    )(h, mods, *weights)


def _hg_scan_kernel(q_ref, i_ref, lf_ref, o_ref, st_ref, *, rev):
    @pl.when(pl.program_id(1) == 0)
    def _():
        st_ref[...] = jnp.zeros_like(st_ref)

    q = q_ref[0]
    v = i_ref[0]
    lf = lf_ref[0, 0]
    c = q.shape[0]
    row = lax.broadcasted_iota(jnp.int32, (c, c), 0)
    col = lax.broadcasted_iota(jnp.int32, (c, c), 1)
    tri = jnp.where((col >= row) if rev else (col <= row), 1.0, 0.0).astype(BF16)
    hi = lf.astype(BF16)
    r1 = lf - hi.astype(F32)
    mid = r1.astype(BF16)
    lo = (r1 - mid.astype(F32)).astype(BF16)
    cum = _dot(tri, hi) + _dot(tri, mid) + _dot(tri, lo)
    key = 1.0 - jnp.exp(lf)
    tot = cum[0:1] if rev else cum[c - 1:c]
    q_dec = (q * jnp.exp(cum)).astype(BF16)
    k_dec = (key * jnp.exp(tot - cum)).astype(BF16)
    v16 = v.astype(BF16)
    rows = lax.broadcasted_iota(jnp.int32, (c, 1), 0)
    cols = lax.broadcasted_iota(jnp.int32, (1, HG_SUB), 1)
    for h in range(N_HEADS):
        sl = slice(h * LANES, (h + 1) * LANES)
        st = st_ref[h]
        o = _dot_nt(q_dec[:, sl], st.astype(BF16))
        for jb in range(c // HG_SUB):
            r0 = jb * HG_SUB
            mid_row = r0 + HG_SUB // 2
            ref = cum[mid_row:mid_row + 1, sl]
            kt = (key[r0:r0 + HG_SUB, sl] * jnp.exp(ref - cum[r0:r0 + HG_SUB, sl])).astype(BF16)
            used = (rows < r0 + HG_SUB) if rev else (rows >= r0)
            qt = (q[:, sl] * jnp.exp(jnp.where(used, cum[:, sl] - ref, 0.0))).astype(BF16)
            att = _dot_nt(qt, kt)
            keep = (rows <= r0 + cols) if rev else (rows >= r0 + cols)
            att = jnp.where(keep, att, 0.0).astype(BF16)
            o = o + _dot(att, v16[r0:r0 + HG_SUB, sl])
        o_ref[0, :, sl] = o
        st_ref[h] = st * jnp.exp(tot[:, sl]) + _dot_tn(v16[:, sl], k_dec[:, sl])


def _hg_scan_call(q, i, lf, seq, rev):
    b, nt, d = q.shape
    c = HG_CHUNK
    n_lat = seq // c
    n_ctx = (nt - seq) // c
    dirn = 1 if rev else 0

    def blk(j):
        if rev:
            return n_lat + n_ctx - 1 - j
        return jnp.where(j < n_ctx, n_lat + j, j - n_ctx)

    return pl.pallas_call(
        functools.partial(_hg_scan_kernel, rev=rev),
        grid=(b, nt // c),
        in_specs=[pl.BlockSpec((1, c, d), lambda bb, j: (bb, blk(j), 0)),
                  pl.BlockSpec((1, c, d), lambda bb, j: (bb, blk(j), 0)),
                  pl.BlockSpec((1, 1, c, d), lambda bb, j: (bb, dirn, blk(j), 0))],
        out_specs=pl.BlockSpec((1, c, d), lambda bb, j: (bb, blk(j), 0)),
        out_shape=jax.ShapeDtypeStruct((b, nt, d), F32),
        scratch_shapes=[pltpu.VMEM((N_HEADS, LANES, LANES), F32)],
        compiler_params=_cparams("parallel", "arbitrary"),
        name="hg_scan_rev" if rev else "hg_scan_fwd",
    )(q, i, lf)


def _post_kernel(*refs, mode, d_ff):
    if mode == "hg":
        x_ref, mod_ref, of_ref, ob_ref, g_ref, onw_ref, wo_ref, n2_ref, w13_ref, w2_ref, out_ref = refs
        o = _group_rms(of_ref[0] + ob_ref[0], onw_ref[...], LANES) * _silu(g_ref[0])
        o = o.astype(BF16)
    else:
        x_ref, mod_ref, o_ref, wo_ref, n2_ref, w13_ref, w2_ref, out_ref = refs
        o = o_ref[0]
    m = mod_ref[0, 0]
    x1 = x_ref[0] + m[2:3] * _dot(o, wo_ref[...])
    a2 = _norm_mod(x1, n2_ref[...], m[3:4], m[4:5]).astype(BF16)
    hc = _dot(a2, w13_ref[...])
    hid = (_silu(hc[:, :d_ff]) * hc[:, d_ff:]).astype(BF16)
    out_ref[0] = x1 + m[5:6] * _dot(hid, w2_ref[...])


def _post_call(h, mods, layer, n_lat_tiles, n_out_rows, mixer_out, w_o, n2, w13, w2, onw=None):
    b, nt, d = h.shape
    tm = TOKEN_TILE
    d_ff = w2.shape[0]
    mode = "hg" if onw is not None else "plain"
    acts = list(mixer_out)
    weights = ([onw] if mode == "hg" else []) + [w_o, n2, w13, w2]
    return pl.pallas_call(
        functools.partial(_post_kernel, mode=mode, d_ff=d_ff),
        grid=(b, n_out_rows // tm),
        in_specs=[_tok_spec(d), _mod_spec(layer, n_lat_tiles, b, d)]
        + [_tok_spec(d) for _ in acts] + [_const_spec(w.shape) for w in weights],
        out_specs=_tok_spec(d),
        out_shape=jax.ShapeDtypeStruct((b, n_out_rows, d), F32),
        compiler_params=_cparams("parallel", "parallel"),
        name="post_" + mode,
    )(h, mods, *acts, *weights)


def _rope_tables(seq, ctx_len, reps):
    rows = seq // GRID_W
    row = jnp.repeat(jnp.arange(rows, dtype=F32), GRID_W)
    col = jnp.tile(jnp.arange(GRID_W, dtype=F32), rows)
    axis_dim = ROPE_DIM // 2
    inv_freq = jnp.power(ROPE_BASE, -jnp.arange(0, axis_dim, 2, dtype=F32) / axis_dim)
    ang_r = row[:, None] * inv_freq
    ang_c = col[:, None] * inv_freq
    ang = jnp.concatenate([ang_r, ang_r, ang_c, ang_c], axis=-1)
    cos = jnp.concatenate([jnp.cos(ang), jnp.ones((ctx_len, ROPE_DIM), F32)], axis=0)
    sin = jnp.concatenate([jnp.sin(ang), jnp.zeros((ctx_len, ROPE_DIM), F32)], axis=0)
    return jnp.tile(cos, (1, reps)), jnp.tile(sin, (1, reps))


def _row(w, reps=1):
    return jnp.tile(w.astype(F32), reps)[None, :]


def _pad_groups(w, group, n_groups, to):
    lead = w.shape[:-1]
    w = w.reshape(lead + (n_groups, group))
    w = jnp.pad(w, [(0, 0)] * len(lead) + [(0, 0), (0, to - group)])
    return w.reshape(lead + (n_groups * to,))


def _mla_params(j, mla_w_in, mla_q_norm_w, mla_kv_norm_w, mla_w_uq, mla_w_ukv, mla_qn_w, mla_qr_w,
                mla_kn_w, mla_kr_w):
    hq = MLA_NOPE + MLA_ROPE
    w_uq = mla_w_uq[j].reshape(MLA_Q_RANK, N_HEADS, hq)
    w_ukv = mla_w_ukv[j].reshape(MLA_KV_RANK, N_HEADS, MLA_NOPE + MLA_V)
    return {
        "w_in": jnp.pad(mla_w_in[j], ((0, 0), (0, LANES - MLA_ROPE))).astype(BF16),
        "q_norm_w": _row(mla_q_norm_w[j]),
        "kv_norm_w": _row(mla_kv_norm_w[j]),
        "w_uq_n": w_uq[:, :, :MLA_NOPE].reshape(MLA_Q_RANK, -1).astype(BF16),
        "w_uq_r": _pad_groups(w_uq[:, :, MLA_NOPE:].reshape(MLA_Q_RANK, -1), MLA_ROPE, N_HEADS,
                              LANES).astype(BF16),
        "w_uk": w_ukv[:, :, :MLA_NOPE].reshape(MLA_KV_RANK, -1).astype(BF16),
        "w_uv": w_ukv[:, :, MLA_NOPE:].reshape(MLA_KV_RANK, -1).astype(BF16),
        "qn_w": _row(mla_qn_w[j], N_HEADS),
        "qr_w": _row(_pad_groups(mla_qr_w[j], MLA_ROPE, 1, LANES), N_HEADS),
        "kn_w": _row(mla_kn_w[j], N_HEADS),
        "kr_w": _row(_pad_groups(mla_kr_w[j], MLA_ROPE, 1, LANES)),
    }


def kernel(x, c, ctx, c_ctx, ada_w, ada_b, norm1_w, norm2_w, ffn_w1, ffn_w3, ffn_w2, mla_w_in, mla_q_norm_w, mla_kv_norm_w, mla_w_uq, mla_w_ukv, mla_qn_w, mla_qr_w, mla_kn_w, mla_kr_w, mla_w_o, hg_w_in, hg_lb_logits, hg_o_norm_w, hg_w_o, df_w_qkv, df_qn_w, df_kn_w, df_lambda, df_sub_norm_w, df_w_o):
    b, seq, d = x.shape
    ctx_len = ctx.shape[1]
    depth = ada_w.shape[0]
    tm = TOKEN_TILE
    assert d == N_HEADS * LANES and b < 8
    assert seq % tm == 0 and ctx_len % tm == 0 and seq % GRID_W == 0
    assert seq % ctx_len == 0 and seq % min(ATTN_TQ, seq) == 0 and seq % min(ATTN_TKB, seq) == 0
    assert ctx_len % HG_CHUNK == 0 and seq % HG_CHUNK == 0
    nt = seq + ctx_len
    n_lat_tiles = seq // tm

    h = jnp.concatenate([x, ctx], axis=1)
    cvec = jnp.zeros((8, d), F32).at[:b].set(c).at[b].set(c_ctx)
    mods = _ada_call(cvec, ada_w, ada_b).reshape(depth, 8, N_MOD, d)
    cos_t, sin_t = _rope_tables(seq, ctx_len, LANES // ROPE_DIM)

    for i in range(depth):
        kind, j = i % N_MIXERS, i // N_MIXERS
        last = i == depth - 1
        n_out_rows = seq if last else nt
        n1 = _row(norm1_w[i])
        n2 = _row(norm2_w[i])
        w13 = jnp.concatenate([ffn_w1[i], ffn_w3[i]], axis=1).astype(BF16)
        w2 = ffn_w2[i].astype(BF16)
        onw = None
        if kind == 0:
            p = _mla_params(j, mla_w_in, mla_q_norm_w, mla_kv_norm_w, mla_w_uq, mla_w_ukv, mla_qn_w,
                            mla_qr_w, mla_kn_w, mla_kr_w)
            q, k, v = _mla_pre_call(h, mods, i, n_lat_tiles, n1, p, cos_t, sin_t)
            mixer_out = [_attend(q[:, :, None], k, v, seq, not last)]
            w_o = mla_w_o[j].astype(BF16)
        elif kind == 1:
            q, gi, g, lf = _hg_pre_call(h, mods, i, n_lat_tiles, n1, hg_w_in[j].astype(BF16),
                                        hg_lb_logits.astype(F32))
            mixer_out = [_hg_scan_call(q, gi, lf, seq, False), _hg_scan_call(q, gi, lf, seq, True), g]
            onw = _row(hg_o_norm_w[j], N_HEADS)
            w_o = hg_w_o[j].astype(BF16)
        else:
            p = {"w_qkv": df_w_qkv[j].astype(BF16), "qn_w": _row(df_qn_w[j], d // ROPE_DIM),
                 "kn_w": _row(df_kn_w[j], d // ROPE_DIM)}
            q, k, v = _df_pre_call(h, mods, i, n_lat_tiles, n1, p, cos_t, sin_t)
            lam_init = 0.8 - 0.6 * math.exp(-0.3 * i)
            mixer_out = [_attend(q, k, v, seq, not last,
                                 extra=(df_lambda[j].astype(F32), _row(df_sub_norm_w[j])),
                                 lam_init=lam_init)]
            w_o = df_w_o[j].astype(BF16)
        h = _post_call(h, mods, i, n_lat_tiles, n_out_rows, mixer_out, w_o, n2, w13, w2, onw)
    return h
```

```python
import functools
import math

import jax
import jax.numpy as jnp
from jax import lax
from jax.experimental import pallas as pl
from jax.experimental.pallas import tpu as pltpu

F32 = jnp.float32
BF16 = jnp.bfloat16

EPS = 1e-6
ROPE_BASE = 10000.0
GRID_W = 64
N_MOD = 6
N_MIXERS = 3

N_HEADS = 8
LANES = 128
MLA_NOPE = 128
MLA_ROPE = 64
MLA_V = 128
MLA_Q_RANK = 384
MLA_KV_RANK = 256
MLA_DK = 256
ROPE_DIM = 64
ROPE_QUARTER = ROPE_DIM // 4

TOKEN_TILE = 256
ATTN_TQ = 512
ATTN_TKB = 2048
ATTN_DT = jnp.float8_e4m3fn
P_SHIFT = 8.0
LOG2_E = math.log2(math.e)
HG_CHUNK = 128
HG_SUB = 16
HG_BATCH_GROUP = 2
VMEM_LIMIT_BYTES = 56 * 1024 * 1024


def _cparams(*sem):
    return pltpu.CompilerParams(dimension_semantics=sem, vmem_limit_bytes=VMEM_LIMIT_BYTES)


def _const_spec(shape):
    nd = len(shape)
    return pl.BlockSpec(shape, lambda *_: (0,) * nd, pipeline_mode=pl.Buffered(1))


def _dot(a, b):
    return jnp.dot(a, b, preferred_element_type=F32)


def _dot_nt(a, b):
    return lax.dot_general(a, b, (((1,), (1,)), ((), ())), preferred_element_type=F32)


def _dot_tn(a, b):
    return lax.dot_general(a, b, (((0,), (0,)), ((), ())), preferred_element_type=F32)


def _silu(t):
    return t * jax.nn.sigmoid(t)


def _rms(t, w):
    return t * lax.rsqrt(jnp.mean(t * t, axis=-1, keepdims=True) + EPS) * w


def _norm_mod(t, w, shift, scale):
    return _rms(t, w) * (1.0 + scale) + shift


def _group_rms(t, w, group):
    width = t.shape[-1]
    lane = lax.broadcasted_iota(jnp.int32, (1, LANES), 1)
    low = lane < 64
    outs = []
    for p in range(width // LANES):
        ts = t[:, p * LANES:(p + 1) * LANES]
        sq = ts * ts
        if group == LANES:
            ms = jnp.sum(sq, axis=-1, keepdims=True) * (1.0 / LANES)
        else:
            s_lo = jnp.sum(jnp.where(low, sq, 0.0), axis=-1, keepdims=True)
            s_hi = jnp.sum(jnp.where(low, 0.0, sq), axis=-1, keepdims=True)
            ms = jnp.where(low, s_lo, s_hi) * (1.0 / 64)
        outs.append(ts * lax.rsqrt(ms + EPS))
    return jnp.concatenate(outs, axis=-1) * w


def _ones_column(rows):
    lane = lax.broadcasted_iota(jnp.int32, (rows, LANES), 1)
    return jnp.where(lane == 0, 1.0, 0.0).astype(ATTN_DT)


def _rope(t, cos, sin):
    width = t.shape[-1]
    lane = lax.broadcasted_iota(jnp.int32, (1, width), 1)
    up = pltpu.roll(t, width - ROPE_QUARTER, 1)
    dn = pltpu.roll(t, ROPE_QUARTER, 1)
    rot = jnp.where((lane % (2 * ROPE_QUARTER)) < ROPE_QUARTER, -up, dn)
    return t * cos + rot * sin


def _ada_kernel(c_ref, w_ref, b_ref, o_ref):
    s = _silu(c_ref[...]).astype(BF16)
    o_ref[0] = _dot(s, w_ref[0].astype(BF16)) + b_ref[0]


def _ada_call(cvec, ada_w, ada_b):
    depth, d, n = ada_w.shape
    tn = 768
    return pl.pallas_call(
        _ada_kernel,
        grid=(depth, n // tn),
        in_specs=[
            pl.BlockSpec((8, d), lambda l, j: (0, 0)),
            pl.BlockSpec((1, d, tn), lambda l, j: (l, 0, j)),
            pl.BlockSpec((1, 1, tn), lambda l, j: (l, 0, j)),
        ],
        out_specs=pl.BlockSpec((1, 8, tn), lambda l, j: (l, 0, j)),
        out_shape=jax.ShapeDtypeStruct((depth, 8, n), F32),
        compiler_params=_cparams("parallel", "parallel"),
        name="ada_mod",
    )(cvec, ada_w, ada_b.reshape(depth, 1, n))


def _tok_spec(width, tm=TOKEN_TILE):
    return pl.BlockSpec((1, tm, width), lambda b, i: (b, i, 0))


def _mod_spec(layer, n_lat_tiles, ctx_row, d):
    return pl.BlockSpec((1, 1, N_MOD, d),
                        lambda b, i: (layer, jnp.where(i < n_lat_tiles, b, ctx_row), 0, 0))


def _batch_of(a):
    return (a[0] if isinstance(a, tuple) else a).shape[0]


def _token_inputs(arrs, n_lat_tiles, tm=TOKEN_TILE):
    specs, flat, groups = [], [], []
    for a in arrs:
        if isinstance(a, tuple):
            w = a[0].shape[-1]
            specs.append(pl.BlockSpec((1, tm, w), lambda b, i: (b, jnp.minimum(i, n_lat_tiles - 1), 0)))
            specs.append(pl.BlockSpec((1, tm, w), lambda b, i: (b, jnp.maximum(i - n_lat_tiles, 0), 0)))
            flat += list(a)
            groups.append(2)
        else:
            specs.append(_tok_spec(a.shape[-1], tm))
            flat.append(a)
            groups.append(1)
    return specs, flat, tuple(groups)


def _token_tiles(refs, groups, n_lat_tiles):
    is_ctx = pl.program_id(1) >= n_lat_tiles
    tiles, k = [], 0
    for g in groups:
        tiles.append(refs[k][0] if g == 1 else jnp.where(is_ctx, refs[k + 1][0], refs[k][0]))
        k += g
    return tiles, refs[k:]


def _head_spec(width, tm=TOKEN_TILE):
    return pl.BlockSpec((1, N_HEADS, tm, width), lambda b, i: (b, 0, i, 0))


def _mla_pre_kernel(*refs, scale, groups, n_lat_tiles):
    (x,), refs = _token_tiles(refs, groups, n_lat_tiles)
    (mod_ref, n1_ref, win_ref, qnw_ref, kvnw_ref, wuqn_ref, wuqr_ref, wk_ref, wv_ref, qn_ref, qr_ref,
     kn_ref, kr_ref, cos_ref, sin_ref, q_ref, k_ref, v_ref) = refs
    m = mod_ref[0, 0]
    a = _norm_mod(x, n1_ref[...], m[0:1], m[1:2]).astype(BF16)
    t = _dot(a, win_ref[...])
    cq = _rms(t[:, :MLA_Q_RANK], qnw_ref[...]).astype(BF16)
    ckv = _rms(t[:, MLA_Q_RANK:MLA_Q_RANK + MLA_KV_RANK], kvnw_ref[...]).astype(BF16)
    kr = t[:, MLA_Q_RANK + MLA_KV_RANK:]
    cos1 = cos_ref[...]
    sin1 = sin_ref[...]
    cos = jnp.concatenate([cos1] * N_HEADS, axis=-1)
    sin = jnp.concatenate([sin1] * N_HEADS, axis=-1)
    kr = kr * lax.rsqrt(jnp.sum(kr * kr, axis=-1, keepdims=True) * (1.0 / MLA_ROPE) + EPS) * kr_ref[...]
    kr = _rope(kr, cos1, sin1).astype(ATTN_DT)
    qn = _group_rms(_dot(cq, wuqn_ref[...]), qn_ref[...], LANES) * scale
    qr = _group_rms(_dot(cq, wuqr_ref[...]), qr_ref[...], 64)
    qr = _rope(qr, cos, sin) * scale
    kn = _group_rms(_dot(ckv, wk_ref[...]), kn_ref[...], LANES)
    v = _dot(ckv, wv_ref[...])
    for h in range(N_HEADS):
        sl = slice(h * LANES, (h + 1) * LANES)
        q_ref[0, h, :, :LANES] = qn[:, sl].astype(ATTN_DT)
        q_ref[0, h, :, LANES:] = qr[:, sl].astype(ATTN_DT)
        k_ref[0, h, :, :LANES] = kn[:, sl].astype(ATTN_DT)
        k_ref[0, h, :, LANES:] = kr
        v_ref[0, h, :, :LANES] = v[:, sl].astype(ATTN_DT)
        v_ref[0, h, :, LANES:] = _ones_column(v.shape[0])


def _mla_pre_call(h, mods, layer, n_lat_tiles, n1, p, cos_t, sin_t):
    b, d = _batch_of(h), mods.shape[-1]
    nt = cos_t.shape[0]
    tm = TOKEN_TILE
    weights = [n1, p["w_in"], p["q_norm_w"], p["kv_norm_w"], p["w_uq_n"], p["w_uq_r"], p["w_uk"],
               p["w_uv"], p["qn_w"], p["qr_w"], p["kn_w"], p["kr_w"]]
    scale = LOG2_E / math.sqrt(MLA_NOPE + MLA_ROPE)
    specs, flat, groups = _token_inputs([h], n_lat_tiles)
    return pl.pallas_call(
        functools.partial(_mla_pre_kernel, scale=scale, groups=groups, n_lat_tiles=n_lat_tiles),
        grid=(b, nt // tm),
        in_specs=specs + [_mod_spec(layer, n_lat_tiles, b, d)]
        + [_const_spec(w.shape) for w in weights]
        + [pl.BlockSpec((tm, LANES), lambda bb, i: (i, 0))] * 2,
        out_specs=[_head_spec(MLA_DK), _head_spec(MLA_DK), _head_spec(MLA_V + LANES)],
        out_shape=[jax.ShapeDtypeStruct((b, N_HEADS, nt, MLA_DK), ATTN_DT),
                   jax.ShapeDtypeStruct((b, N_HEADS, nt, MLA_DK), ATTN_DT),
                   jax.ShapeDtypeStruct((b, N_HEADS, nt, MLA_V + LANES), ATTN_DT)],
        compiler_params=_cparams("parallel", "parallel"),
        name="mla_pre",
    )(*flat, mods, *weights, cos_t, sin_t)


def _df_pre_kernel(x_ref, mod_ref, n1_ref, wqkv_ref, qn_ref, kn_ref, cos_ref, sin_ref,
                   q_ref, k_ref, v_ref, *, scale):
    m = mod_ref[0, 0]
    a = _norm_mod(x_ref[0], n1_ref[...], m[0:1], m[1:2]).astype(BF16)
    width = N_HEADS * LANES
    t = _dot(a, wqkv_ref[...])
    cos = jnp.concatenate([cos_ref[...]] * N_HEADS, axis=-1)
    sin = jnp.concatenate([sin_ref[...]] * N_HEADS, axis=-1)
    q = _rope(_group_rms(t[:, :width], qn_ref[...], 64), cos, sin) * scale
    k = _rope(_group_rms(t[:, width:2 * width], kn_ref[...], 64), cos, sin)
    v = t[:, 2 * width:]
    lane = lax.broadcasted_iota(jnp.int32, (1, LANES), 1)
    low = lane < 64
    for h in range(N_HEADS):
        sl = slice(h * LANES, (h + 1) * LANES)
        qh = q[:, sl]
        q_ref[0, h, 0] = jnp.where(low, qh, 0.0).astype(ATTN_DT)
        q_ref[0, h, 1] = jnp.where(low, 0.0, qh).astype(ATTN_DT)
        k_ref[0, h] = k[:, sl].astype(ATTN_DT)
        v_ref[0, h, :, :LANES] = v[:, sl].astype(ATTN_DT)
        v_ref[0, h, :, LANES:] = _ones_column(v.shape[0])


def _df_pre_call(h, mods, layer, n_lat_tiles, n1, p, cos_t, sin_t):
    b, nt, d = h.shape
    tm = TOKEN_TILE
    width = N_HEADS * LANES
    weights = [n1, p["w_qkv"], p["qn_w"], p["kn_w"]]
    return pl.pallas_call(
        functools.partial(_df_pre_kernel, scale=LOG2_E / math.sqrt(ROPE_DIM)),
        grid=(b, nt // tm),
        in_specs=[_tok_spec(d), _mod_spec(layer, n_lat_tiles, b, d)]
        + [_const_spec(w.shape) for w in weights]
        + [pl.BlockSpec((tm, LANES), lambda bb, i: (i, 0))] * 2,
        out_specs=[pl.BlockSpec((1, N_HEADS, 2, tm, LANES), lambda bb, i: (bb, 0, 0, i, 0)),
                   _head_spec(LANES), _head_spec(2 * LANES)],
        out_shape=[jax.ShapeDtypeStruct((b, N_HEADS, 2, nt, LANES), ATTN_DT),
                   jax.ShapeDtypeStruct((b, N_HEADS, nt, LANES), ATTN_DT),
                   jax.ShapeDtypeStruct((b, N_HEADS, nt, 2 * LANES), ATTN_DT)],
        compiler_params=_cparams("parallel", "parallel"),
        name="df_pre",
    )(h, mods, *weights, cos_t, sin_t)


def _flash_kernel(*refs, n_sub, head_blocks, n_loop, tkb, lam_init):
    if n_sub == 2:
        q_ref, k_ref, v_ref, lam_ref, snw_ref, o_ref = refs
    else:
        q_ref, k_ref, v_ref, o_ref = refs
    qs = [q_ref[0, 0, s] for s in range(n_sub)]
    tq = qs[0].shape[0]
    dv = v_ref.shape[-1] - LANES

    def step(carry, kb, vb):
        new = []
        for s in range(n_sub):
            m_old, c_old, acc = carry[s]
            sc = _dot_nt(qs[s], kb).astype(BF16)
            m_new = jnp.maximum(m_old, jnp.max(sc, axis=-1, keepdims=True).astype(F32))
            c_new = (m_new - P_SHIFT).astype(BF16)
            p = jnp.exp2(sc - c_new).astype(ATTN_DT)
            alpha = jnp.exp2(c_old - c_new.astype(F32))
            new.append((m_new, c_new.astype(F32), alpha * acc + _dot(p, vb)))
        return tuple(new)

    carry = tuple((jnp.full((tq, 1), -jnp.inf, F32), jnp.full((tq, 1), -jnp.inf, F32),
                   jnp.zeros((tq, dv + LANES), F32)) for _ in range(n_sub))
    for start, rows in head_blocks:
        carry = step(carry, k_ref[0, 0, start:start + rows, :], v_ref[0, 0, start:start + rows, :])
    if n_loop:
        def body(j, c):
            off = pl.multiple_of(j * tkb, tkb)
            return step(c, k_ref[0, 0, pl.ds(off, tkb), :], v_ref[0, 0, pl.ds(off, tkb), :])
        carry = lax.fori_loop(0, n_loop, body, carry, unroll=True)
    outs = [acc[:, :dv] / acc[:, dv:dv + 1] for _, _, acc in carry]
    if n_sub == 1:
        o_ref[0] = outs[0].astype(o_ref.dtype)
    else:
        lv = lam_ref[...]
        lam = (jnp.exp(jnp.sum(lv[0:1] * lv[1:2], axis=-1, keepdims=True))
               - jnp.exp(jnp.sum(lv[2:3] * lv[3:4], axis=-1, keepdims=True)) + lam_init)
        o = outs[0] - lam * outs[1]
        o = _rms(o, snw_ref[...]) * (1.0 - lam_init)
        o_ref[0] = o.astype(o_ref.dtype)


def _flash_call(q, k, v, seq, ctx_queries, extra=(), lam_init=0.0):
    b, nh, n_sub, nt, dk = q.shape
    dve = v.shape[-1]
    dv = dve - LANES
    ctx_len = nt - seq
    if ctx_queries:
        tq, n_rows, q0 = ctx_len, ctx_len, seq // ctx_len
        kv_rows, kv0 = ctx_len, seq // ctx_len
        head_blocks, n_loop, tkb = ((0, ctx_len),), 0, 0
    else:
        tq, n_rows, q0 = min(ATTN_TQ, seq), seq, 0
        kv_rows, kv0 = nt, 0
        tkb = min(ATTN_TKB, seq)
        head_blocks, n_loop = ((seq, ctx_len),), seq // tkb
    in_specs = [pl.BlockSpec((1, 1, n_sub, tq, dk), lambda bb, hh, i: (bb, hh, 0, q0 + i, 0)),
                pl.BlockSpec((1, 1, kv_rows, dk), lambda bb, hh, i: (bb, hh, kv0, 0)),
                pl.BlockSpec((1, 1, kv_rows, dve), lambda bb, hh, i: (bb, hh, kv0, 0))]
    in_specs += [_const_spec(e.shape) for e in extra]
    return pl.pallas_call(
        functools.partial(_flash_kernel, n_sub=n_sub, head_blocks=head_blocks, n_loop=n_loop, tkb=tkb,
                          lam_init=lam_init),
        grid=(b, nh, n_rows // tq),
        in_specs=in_specs,
        out_specs=pl.BlockSpec((1, tq, dv), lambda bb, hh, i: (bb, i, hh)),
        out_shape=jax.ShapeDtypeStruct((b, n_rows, nh * dv), BF16),
        compiler_params=_cparams("parallel", "parallel", "arbitrary"),
        name="flash_ctx" if ctx_queries else "flash_lat",
    )(q, k, v, *extra)


def _attend(q, k, v, seq, with_ctx, extra=(), lam_init=0.0):
    o = _flash_call(q, k, v, seq, False, extra, lam_init)
    if with_ctx:
        return (o, _flash_call(q, k, v, seq, True, extra, lam_init))
    return o


def _hg_pre_kernel(x_ref, mod_ref, n1_ref, win_ref, lbl_ref, q_ref, i_ref, g_ref, lf_ref, *, layer):
    m = mod_ref[0, 0]
    a = _norm_mod(x_ref[0], n1_ref[...], m[0:1], m[1:2]).astype(BF16)
    width = N_HEADS * LANES
    t = _dot(a, win_ref[...])
    logits = lbl_ref[...]
    e = jnp.exp(logits - jnp.max(logits, axis=0, keepdims=True))
    sm = e / jnp.sum(e, axis=0, keepdims=True)
    lb = jnp.zeros_like(sm[0])
    for j in range(1, layer + 1):
        lb = lb + sm[j]
    q_ref[0] = t[:, :width]
    i_ref[0] = t[:, 3 * width:4 * width]
    g_ref[0] = t[:, 4 * width:]
    for dirn in range(2):
        z = t[:, (1 + dirn) * width:(2 + dirn) * width]
        lbd = lb[dirn:dirn + 1]
        lf_ref[0, dirn] = jnp.log(lbd + (1.0 - lbd) * jax.nn.sigmoid(z))


def _hg_pre_call(h, mods, layer, n_lat_tiles, n1, w_in, lb_logits):
    b, nt, d = h.shape
    tm = TOKEN_TILE
    weights = [n1, w_in, lb_logits]
    tok = jax.ShapeDtypeStruct((b, nt, d), F32)
    return pl.pallas_call(
        functools.partial(_hg_pre_kernel, layer=layer),
        grid=(b, nt // tm),
        in_specs=[_tok_spec(d), _mod_spec(layer, n_lat_tiles, b, d)] + [_const_spec(w.shape) for w in weights],
        out_specs=[_tok_spec(d), _tok_spec(d), _tok_spec(d),
                   pl.BlockSpec((1, 2, tm, d), lambda bb, i: (bb, 0, i, 0))],
        out_shape=[tok, tok, tok, jax.ShapeDtypeStruct((b, 2, nt, d), F32)],
        compiler_params=_cparams("parallel", "parallel"),
        name="hg_pre",
    )(h, mods, *weights)


def _hg_scan_kernel(q_ref, i_ref, lf_ref, o_ref, st_ref, *, rev):
    @pl.when(pl.program_id(1) == 0)
    def _():
        st_ref[...] = jnp.zeros_like(st_ref)

    c = q_ref.shape[1]
    row = lax.broadcasted_iota(jnp.int32, (c, c), 0)
    col = lax.broadcasted_iota(jnp.int32, (c, c), 1)
    tri = jnp.where((col >= row) if rev else (col <= row), 1.0, 0.0).astype(BF16)
    rows = lax.broadcasted_iota(jnp.int32, (c, 1), 0)
    cols = lax.broadcasted_iota(jnp.int32, (1, HG_SUB), 1)
    for bi in range(q_ref.shape[0]):
        q = q_ref[bi]
        v = i_ref[bi]
        lf = lf_ref[bi, 0]
        hi = lf.astype(BF16)
        r1 = lf - hi.astype(F32)
        mid = r1.astype(BF16)
        lo = (r1 - mid.astype(F32)).astype(BF16)
        cum = _dot(tri, hi) + _dot(tri, mid) + _dot(tri, lo)
        key = 1.0 - jnp.exp(lf)
        tot = cum[0:1] if rev else cum[c - 1:c]
        q_dec = (q * jnp.exp(cum)).astype(BF16)
        k_dec = (key * jnp.exp(tot - cum)).astype(BF16)
        v16 = v.astype(BF16)
        for h in range(N_HEADS):
            sl = slice(h * LANES, (h + 1) * LANES)
            st = st_ref[bi, h]
            o = _dot_nt(q_dec[:, sl], st.astype(BF16))
            for jb in range(c // HG_SUB):
                r0 = jb * HG_SUB
                mid_row = r0 + HG_SUB // 2
                ref = cum[mid_row:mid_row + 1, sl]
                kt = (key[r0:r0 + HG_SUB, sl] * jnp.exp(ref - cum[r0:r0 + HG_SUB, sl])).astype(BF16)
                used = (rows < r0 + HG_SUB) if rev else (rows >= r0)
                qt = (q[:, sl] * jnp.exp(jnp.where(used, cum[:, sl] - ref, 0.0))).astype(BF16)
                att = _dot_nt(qt, kt)
                keep = (rows <= r0 + cols) if rev else (rows >= r0 + cols)
                att = jnp.where(keep, att, 0.0).astype(BF16)
                o = o + _dot(att, v16[r0:r0 + HG_SUB, sl])
            o_ref[bi, :, sl] = o
            st_ref[bi, h] = st * jnp.exp(tot[:, sl]) + _dot_tn(v16[:, sl], k_dec[:, sl])


def _hg_scan_call(q, i, lf, seq, rev):
    b, nt, d = q.shape
    c = HG_CHUNK
    n_lat = seq // c
    n_ctx = (nt - seq) // c
    dirn = 1 if rev else 0
    nb = HG_BATCH_GROUP if b % HG_BATCH_GROUP == 0 else 1

    def blk(j):
        if rev:
            return n_lat + n_ctx - 1 - j
        return jnp.where(j < n_ctx, n_lat + j, j - n_ctx)

    return pl.pallas_call(
        functools.partial(_hg_scan_kernel, rev=rev),
        grid=(b // nb, nt // c),
        in_specs=[pl.BlockSpec((nb, c, d), lambda bb, j: (bb, blk(j), 0)),
                  pl.BlockSpec((nb, c, d), lambda bb, j: (bb, blk(j), 0)),
                  pl.BlockSpec((nb, 1, c, d), lambda bb, j: (bb, dirn, blk(j), 0))],
        out_specs=pl.BlockSpec((nb, c, d), lambda bb, j: (bb, blk(j), 0)),
        out_shape=jax.ShapeDtypeStruct((b, nt, d), F32),
        scratch_shapes=[pltpu.VMEM((nb, N_HEADS, LANES, LANES), F32)],
        compiler_params=_cparams("parallel", "arbitrary"),
        name="hg_scan_rev" if rev else "hg_scan_fwd",
    )(q, i, lf)


def _post_kernel(*refs, mode, d_ff, groups, n_lat_tiles):
    tiles, refs = _token_tiles(refs, groups, n_lat_tiles)
    if mode == "hg":
        x, o_f, o_b, g = tiles
        mod_ref, onw_ref, wo_ref, n2_ref, w13_ref, w2_ref, out_ref = refs
        o = (_group_rms(o_f + o_b, onw_ref[...], LANES) * _silu(g)).astype(BF16)
    else:
        x, o = tiles
        mod_ref, wo_ref, n2_ref, w13_ref, w2_ref, out_ref = refs
    m = mod_ref[0, 0]
    x1 = x + m[2:3] * _dot(o, wo_ref[...])
    a2 = _norm_mod(x1, n2_ref[...], m[3:4], m[4:5]).astype(BF16)
    hc = _dot(a2, w13_ref[...])
    hid = (_silu(hc[:, :d_ff]) * hc[:, d_ff:]).astype(BF16)
    out_ref[0] = x1 + m[5:6] * _dot(hid, w2_ref[...])


def _post_call(h, mods, layer, n_lat_tiles, n_out_rows, mixer_out, w_o, n2, w13, w2, onw=None):
    d = mods.shape[-1]
    b = _batch_of(h)
    tm = TOKEN_TILE
    d_ff = w2.shape[0]
    mode = "hg" if onw is not None else "plain"
    specs, flat, groups = _token_inputs([h] + list(mixer_out), n_lat_tiles)
    weights = ([onw] if mode == "hg" else []) + [w_o, n2, w13, w2]
    return pl.pallas_call(
        functools.partial(_post_kernel, mode=mode, d_ff=d_ff, groups=groups, n_lat_tiles=n_lat_tiles),
        grid=(b, n_out_rows // tm),
        in_specs=specs + [_mod_spec(layer, n_lat_tiles, b, d)] + [_const_spec(w.shape) for w in weights],
        out_specs=_tok_spec(d),
        out_shape=jax.ShapeDtypeStruct((b, n_out_rows, d), F32),
        compiler_params=_cparams("parallel", "parallel"),
        name="post_" + mode,
    )(*flat, mods, *weights)


def _rope_tables(seq, ctx_len, reps):
    rows = seq // GRID_W
    row = jnp.repeat(jnp.arange(rows, dtype=F32), GRID_W)
    col = jnp.tile(jnp.arange(GRID_W, dtype=F32), rows)
    axis_dim = ROPE_DIM // 2
    inv_freq = jnp.power(ROPE_BASE, -jnp.arange(0, axis_dim, 2, dtype=F32) / axis_dim)
    ang_r = row[:, None] * inv_freq
    ang_c = col[:, None] * inv_freq
    ang = jnp.concatenate([ang_r, ang_r, ang_c, ang_c], axis=-1)
    cos = jnp.concatenate([jnp.cos(ang), jnp.ones((ctx_len, ROPE_DIM), F32)], axis=0)
    sin = jnp.concatenate([jnp.sin(ang), jnp.zeros((ctx_len, ROPE_DIM), F32)], axis=0)
    return jnp.tile(cos, (1, reps)), jnp.tile(sin, (1, reps))


def _row(w, reps=1):
    return jnp.tile(w.astype(F32), reps)[None, :]


def _pad_groups(w, group, n_groups, to):
    lead = w.shape[:-1]
    w = w.reshape(lead + (n_groups, group))
    w = jnp.pad(w, [(0, 0)] * len(lead) + [(0, 0), (0, to - group)])
    return w.reshape(lead + (n_groups * to,))


def _mla_params(j, mla_w_in, mla_q_norm_w, mla_kv_norm_w, mla_w_uq, mla_w_ukv, mla_qn_w, mla_qr_w,
                mla_kn_w, mla_kr_w):
    hq = MLA_NOPE + MLA_ROPE
    w_uq = mla_w_uq[j].reshape(MLA_Q_RANK, N_HEADS, hq)
    w_ukv = mla_w_ukv[j].reshape(MLA_KV_RANK, N_HEADS, MLA_NOPE + MLA_V)
    return {
        "w_in": jnp.pad(mla_w_in[j], ((0, 0), (0, LANES - MLA_ROPE))).astype(BF16),
        "q_norm_w": _row(mla_q_norm_w[j]),
        "kv_norm_w": _row(mla_kv_norm_w[j]),
        "w_uq_n": w_uq[:, :, :MLA_NOPE].reshape(MLA_Q_RANK, -1).astype(BF16),
        "w_uq_r": _pad_groups(w_uq[:, :, MLA_NOPE:].reshape(MLA_Q_RANK, -1), MLA_ROPE, N_HEADS,
                              LANES).astype(BF16),
        "w_uk": w_ukv[:, :, :MLA_NOPE].reshape(MLA_KV_RANK, -1).astype(BF16),
        "w_uv": w_ukv[:, :, MLA_NOPE:].reshape(MLA_KV_RANK, -1).astype(BF16),
        "qn_w": _row(mla_qn_w[j], N_HEADS),
        "qr_w": _row(_pad_groups(mla_qr_w[j], MLA_ROPE, 1, LANES), N_HEADS),
        "kn_w": _row(mla_kn_w[j], N_HEADS),
        "kr_w": _row(_pad_groups(mla_kr_w[j], MLA_ROPE, 1, LANES)),
    }


def kernel(x, c, ctx, c_ctx, ada_w, ada_b, norm1_w, norm2_w, ffn_w1, ffn_w3, ffn_w2, mla_w_in, mla_q_norm_w, mla_kv_norm_w, mla_w_uq, mla_w_ukv, mla_qn_w, mla_qr_w, mla_kn_w, mla_kr_w, mla_w_o, hg_w_in, hg_lb_logits, hg_o_norm_w, hg_w_o, df_w_qkv, df_qn_w, df_kn_w, df_lambda, df_sub_norm_w, df_w_o):
    b, seq, d = x.shape
    ctx_len = ctx.shape[1]
    depth = ada_w.shape[0]
    tm = TOKEN_TILE
    assert d == N_HEADS * LANES and b < 8
    assert seq % tm == 0 and ctx_len % tm == 0 and seq % GRID_W == 0
    assert seq % ctx_len == 0 and seq % min(ATTN_TQ, seq) == 0 and seq % min(ATTN_TKB, seq) == 0
    assert ctx_len % HG_CHUNK == 0 and seq % HG_CHUNK == 0
    nt = seq + ctx_len
    n_lat_tiles = seq // tm

    h = (x, ctx)
    cvec = jnp.zeros((8, d), F32).at[:b].set(c).at[b].set(c_ctx)
    mods = _ada_call(cvec, ada_w, ada_b).reshape(depth, 8, N_MOD, d)
    cos_t, sin_t = _rope_tables(seq, ctx_len, LANES // ROPE_DIM)

    for i in range(depth):
        kind, j = i % N_MIXERS, i // N_MIXERS
        last = i == depth - 1
        n_out_rows = seq if last else nt
        n1 = _row(norm1_w[i])
        n2 = _row(norm2_w[i])
        w13 = jnp.concatenate([ffn_w1[i], ffn_w3[i]], axis=1).astype(BF16)
        w2 = ffn_w2[i].astype(BF16)
        onw = None
        if kind == 0:
            p = _mla_params(j, mla_w_in, mla_q_norm_w, mla_kv_norm_w, mla_w_uq, mla_w_ukv, mla_qn_w,
                            mla_qr_w, mla_kn_w, mla_kr_w)
            q, k, v = _mla_pre_call(h, mods, i, n_lat_tiles, n1, p, cos_t, sin_t)
            mixer_out = [_attend(q[:, :, None], k, v, seq, not last)]
            w_o = mla_w_o[j].astype(BF16)
        elif kind == 1:
            q, gi, g, lf = _hg_pre_call(h, mods, i, n_lat_tiles, n1, hg_w_in[j].astype(BF16),
                                        hg_lb_logits.astype(F32))
            mixer_out = [_hg_scan_call(q, gi, lf, seq, False), _hg_scan_call(q, gi, lf, seq, True), g]
            onw = _row(hg_o_norm_w[j], N_HEADS)
            w_o = hg_w_o[j].astype(BF16)
        else:
            p = {"w_qkv": df_w_qkv[j].astype(BF16), "qn_w": _row(df_qn_w[j], d // ROPE_DIM),
                 "kn_w": _row(df_kn_w[j], d // ROPE_DIM)}
            q, k, v = _df_pre_call(h, mods, i, n_lat_tiles, n1, p, cos_t, sin_t)
            lam_init = 0.8 - 0.6 * math.exp(-0.3 * i)
            mixer_out = [_attend(q, k, v, seq, not last,
                                 extra=(df_lambda[j].astype(F32), _row(df_sub_norm_w[j])),
                                 lam_init=lam_init)]
            w_o = df_w_o[j].astype(BF16)
        h = _post_call(h, mods, i, n_lat_tiles, n_out_rows, mixer_out, w_o, n2, w13, w2, onw)
    return h
```

```python
import functools
import math

import jax
import jax.numpy as jnp
from jax import lax
from jax.experimental import pallas as pl
from jax.experimental.pallas import tpu as pltpu

F32 = jnp.float32
BF16 = jnp.bfloat16

EPS = 1e-6
ROPE_BASE = 10000.0
GRID_W = 64
N_MOD = 6
N_MIXERS = 3

N_HEADS = 8
LANES = 128
MLA_NOPE = 128
MLA_ROPE = 64
MLA_V = 128
MLA_Q_RANK = 384
MLA_KV_RANK = 256
MLA_DK = 256
ROPE_DIM = 64
ROPE_QUARTER = ROPE_DIM // 4

TOKEN_TILE = 256
ATTN_TQ = 512
ATTN_TKB = 2048
ATTN_DT = jnp.float8_e4m3fn
P_SHIFT = 8.0
LOG2_E = math.log2(math.e)
HG_CHUNK = 128
HG_SUB = 16
HG_BATCH_GROUP = 2
VMEM_LIMIT_BYTES = 56 * 1024 * 1024


def _cparams(*sem):
    return pltpu.CompilerParams(dimension_semantics=sem, vmem_limit_bytes=VMEM_LIMIT_BYTES)


def _const_spec(shape):
    nd = len(shape)
    return pl.BlockSpec(shape, lambda *_: (0,) * nd, pipeline_mode=pl.Buffered(1))


def _dot(a, b):
    return jnp.dot(a, b, preferred_element_type=F32)


def _dot_nt(a, b):
    return lax.dot_general(a, b, (((1,), (1,)), ((), ())), preferred_element_type=F32)


def _dot_tn(a, b):
    return lax.dot_general(a, b, (((0,), (0,)), ((), ())), preferred_element_type=F32)


def _silu(t):
    return t * jax.nn.sigmoid(t)


def _rms(t, w):
    return t * lax.rsqrt(jnp.mean(t * t, axis=-1, keepdims=True) + EPS) * w


def _norm_mod(t, w, shift, scale):
    return _rms(t, w) * (1.0 + scale) + shift


def _group_rms(t, w, group):
    width = t.shape[-1]
    lane = lax.broadcasted_iota(jnp.int32, (1, LANES), 1)
    low = lane < 64
    outs = []
    for p in range(width // LANES):
        ts = t[:, p * LANES:(p + 1) * LANES]
        sq = ts * ts
        if group == LANES:
            ms = jnp.sum(sq, axis=-1, keepdims=True) * (1.0 / LANES)
        else:
            s_lo = jnp.sum(jnp.where(low, sq, 0.0), axis=-1, keepdims=True)
            s_hi = jnp.sum(jnp.where(low, 0.0, sq), axis=-1, keepdims=True)
            ms = jnp.where(low, s_lo, s_hi) * (1.0 / 64)
        outs.append(ts * lax.rsqrt(ms + EPS))
    return jnp.concatenate(outs, axis=-1) * w


def _ones_column(rows):
    lane = lax.broadcasted_iota(jnp.int32, (rows, LANES), 1)
    return jnp.where(lane == 0, 1.0, 0.0).astype(ATTN_DT)


def _rope(t, cos, sin):
    width = t.shape[-1]
    lane = lax.broadcasted_iota(jnp.int32, (1, width), 1)
    up = pltpu.roll(t, width - ROPE_QUARTER, 1)
    dn = pltpu.roll(t, ROPE_QUARTER, 1)
    rot = jnp.where((lane % (2 * ROPE_QUARTER)) < ROPE_QUARTER, -up, dn)
    return t * cos + rot * sin


def _ada_kernel(c_ref, w_ref, b_ref, o_ref):
    s = _silu(c_ref[...]).astype(BF16)
    o_ref[0] = _dot(s, w_ref[0].astype(BF16)) + b_ref[0]


def _ada_call(cvec, ada_w, ada_b):
    depth, d, n = ada_w.shape
    tn = 768
    return pl.pallas_call(
        _ada_kernel,
        grid=(depth, n // tn),
        in_specs=[
            pl.BlockSpec((8, d), lambda l, j: (0, 0)),
            pl.BlockSpec((1, d, tn), lambda l, j: (l, 0, j)),
            pl.BlockSpec((1, 1, tn), lambda l, j: (l, 0, j)),
        ],
        out_specs=pl.BlockSpec((1, 8, tn), lambda l, j: (l, 0, j)),
        out_shape=jax.ShapeDtypeStruct((depth, 8, n), F32),
        compiler_params=_cparams("parallel", "parallel"),
        name="ada_mod",
    )(cvec, ada_w, ada_b.reshape(depth, 1, n))


def _tok_spec(width, tm=TOKEN_TILE):
    return pl.BlockSpec((1, tm, width), lambda b, i: (b, i, 0))


def _mod_spec(layer, n_lat_tiles, ctx_row, d):
    return pl.BlockSpec((1, 1, N_MOD, d),
                        lambda b, i: (layer, jnp.where(i < n_lat_tiles, b, ctx_row), 0, 0))


def _batch_of(a):
    return (a[0] if isinstance(a, tuple) else a).shape[0]


def _token_inputs(arrs, n_lat_tiles, tm=TOKEN_TILE):
    specs, flat, groups = [], [], []
    for a in arrs:
        if isinstance(a, tuple):
            w = a[0].shape[-1]
            specs.append(pl.BlockSpec((1, tm, w), lambda b, i: (b, jnp.minimum(i, n_lat_tiles - 1), 0)))
            specs.append(pl.BlockSpec((1, tm, w), lambda b, i: (b, jnp.maximum(i - n_lat_tiles, 0), 0)))
            flat += list(a)
            groups.append(2)
        else:
            specs.append(_tok_spec(a.shape[-1], tm))
            flat.append(a)
            groups.append(1)
    return specs, flat, tuple(groups)


def _token_tiles(refs, groups, n_lat_tiles):
    is_ctx = pl.program_id(1) >= n_lat_tiles
    tiles, k = [], 0
    for g in groups:
        tiles.append(refs[k][0] if g == 1 else jnp.where(is_ctx, refs[k + 1][0], refs[k][0]))
        k += g
    return tiles, refs[k:]


def _head_spec(width, tm=TOKEN_TILE):
    return pl.BlockSpec((1, N_HEADS, tm, width), lambda b, i: (b, 0, i, 0))


def _mla_pre_kernel(*refs, scale, groups, n_lat_tiles):
    (x,), refs = _token_tiles(refs, groups, n_lat_tiles)
    (mod_ref, n1_ref, win_ref, qnw_ref, kvnw_ref, wuqn_ref, wuqr_ref, wk_ref, wv_ref, qn_ref, qr_ref,
     kn_ref, kr_ref, cos_ref, sin_ref, q_ref, k_ref, v_ref) = refs
    m = mod_ref[0, 0]
    a = _norm_mod(x, n1_ref[...], m[0:1], m[1:2]).astype(BF16)
    t = _dot(a, win_ref[...])
    cq = _rms(t[:, :MLA_Q_RANK], qnw_ref[...]).astype(BF16)
    ckv = _rms(t[:, MLA_Q_RANK:MLA_Q_RANK + MLA_KV_RANK], kvnw_ref[...]).astype(BF16)
    kr = t[:, MLA_Q_RANK + MLA_KV_RANK:]
    cos1 = cos_ref[...]
    sin1 = sin_ref[...]
    cos = jnp.concatenate([cos1] * N_HEADS, axis=-1)
    sin = jnp.concatenate([sin1] * N_HEADS, axis=-1)
    kr = kr * lax.rsqrt(jnp.sum(kr * kr, axis=-1, keepdims=True) * (1.0 / MLA_ROPE) + EPS) * kr_ref[...]
    kr = _rope(kr, cos1, sin1).astype(ATTN_DT)
    qn = _group_rms(_dot(cq, wuqn_ref[...]), qn_ref[...], LANES) * scale
    qr = _group_rms(_dot(cq, wuqr_ref[...]), qr_ref[...], 64)
    qr = _rope(qr, cos, sin) * scale
    kn = _group_rms(_dot(ckv, wk_ref[...]), kn_ref[...], LANES)
    v = _dot(ckv, wv_ref[...])
    for h in range(N_HEADS):
        sl = slice(h * LANES, (h + 1) * LANES)
        q_ref[0, h, :, :LANES] = qn[:, sl].astype(ATTN_DT)
        q_ref[0, h, :, LANES:] = qr[:, sl].astype(ATTN_DT)
        k_ref[0, h, :, :LANES] = kn[:, sl].astype(ATTN_DT)
        k_ref[0, h, :, LANES:] = kr
        v_ref[0, h, :, :LANES] = v[:, sl].astype(ATTN_DT)
        v_ref[0, h, :, LANES:] = _ones_column(v.shape[0])


def _mla_pre_call(h, mods, layer, n_lat_tiles, n1, p, cos_t, sin_t):
    b, d = _batch_of(h), mods.shape[-1]
    nt = cos_t.shape[0]
    tm = TOKEN_TILE
    weights = [n1, p["w_in"], p["q_norm_w"], p["kv_norm_w"], p["w_uq_n"], p["w_uq_r"], p["w_uk"],
               p["w_uv"], p["qn_w"], p["qr_w"], p["kn_w"], p["kr_w"]]
    scale = LOG2_E / math.sqrt(MLA_NOPE + MLA_ROPE)
    specs, flat, groups = _token_inputs([h], n_lat_tiles)
    return pl.pallas_call(
        functools.partial(_mla_pre_kernel, scale=scale, groups=groups, n_lat_tiles=n_lat_tiles),
        grid=(b, nt // tm),
        in_specs=specs + [_mod_spec(layer, n_lat_tiles, b, d)]
        + [_const_spec(w.shape) for w in weights]
        + [pl.BlockSpec((tm, LANES), lambda bb, i: (i, 0))] * 2,
        out_specs=[_head_spec(MLA_DK), _head_spec(MLA_DK), _head_spec(MLA_V + LANES)],
        out_shape=[jax.ShapeDtypeStruct((b, N_HEADS, nt, MLA_DK), ATTN_DT),
                   jax.ShapeDtypeStruct((b, N_HEADS, nt, MLA_DK), ATTN_DT),
                   jax.ShapeDtypeStruct((b, N_HEADS, nt, MLA_V + LANES), ATTN_DT)],
        compiler_params=_cparams("parallel", "parallel"),
        name="mla_pre",
    )(*flat, mods, *weights, cos_t, sin_t)


def _df_pre_kernel(x_ref, mod_ref, n1_ref, wqkv_ref, qn_ref, kn_ref, cos_ref, sin_ref,
                   q_ref, k_ref, v_ref, *, scale):
    m = mod_ref[0, 0]
    a = _norm_mod(x_ref[0], n1_ref[...], m[0:1], m[1:2]).astype(BF16)
    width = N_HEADS * LANES
    t = _dot(a, wqkv_ref[...])
    cos = jnp.concatenate([cos_ref[...]] * N_HEADS, axis=-1)
    sin = jnp.concatenate([sin_ref[...]] * N_HEADS, axis=-1)
    q = _rope(_group_rms(t[:, :width], qn_ref[...], 64), cos, sin) * scale
    k = _rope(_group_rms(t[:, width:2 * width], kn_ref[...], 64), cos, sin)
    v = t[:, 2 * width:]
    lane = lax.broadcasted_iota(jnp.int32, (1, LANES), 1)
    low = lane < 64
    for h in range(N_HEADS):
        sl = slice(h * LANES, (h + 1) * LANES)
        qh = q[:, sl]
        q_ref[0, h, 0] = jnp.where(low, qh, 0.0).astype(ATTN_DT)
        q_ref[0, h, 1] = jnp.where(low, 0.0, qh).astype(ATTN_DT)
        k_ref[0, h] = k[:, sl].astype(ATTN_DT)
        v_ref[0, h, :, :LANES] = v[:, sl].astype(ATTN_DT)
        v_ref[0, h, :, LANES:] = _ones_column(v.shape[0])


def _df_pre_call(h, mods, layer, n_lat_tiles, n1, p, cos_t, sin_t):
    b, nt, d = h.shape
    tm = TOKEN_TILE
    width = N_HEADS * LANES
    weights = [n1, p["w_qkv"], p["qn_w"], p["kn_w"]]
    return pl.pallas_call(
        functools.partial(_df_pre_kernel, scale=LOG2_E / math.sqrt(ROPE_DIM)),
        grid=(b, nt // tm),
        in_specs=[_tok_spec(d), _mod_spec(layer, n_lat_tiles, b, d)]
        + [_const_spec(w.shape) for w in weights]
        + [pl.BlockSpec((tm, LANES), lambda bb, i: (i, 0))] * 2,
        out_specs=[pl.BlockSpec((1, N_HEADS, 2, tm, LANES), lambda bb, i: (bb, 0, 0, i, 0)),
                   _head_spec(LANES), _head_spec(2 * LANES)],
        out_shape=[jax.ShapeDtypeStruct((b, N_HEADS, 2, nt, LANES), ATTN_DT),
                   jax.ShapeDtypeStruct((b, N_HEADS, nt, LANES), ATTN_DT),
                   jax.ShapeDtypeStruct((b, N_HEADS, nt, 2 * LANES), ATTN_DT)],
        compiler_params=_cparams("parallel", "parallel"),
        name="df_pre",
    )(h, mods, *weights, cos_t, sin_t)


def _flash_kernel(*refs, n_sub, blocks, lam_init):
    if n_sub == 2:
        q_ref, k_ref, v_ref, lam_ref, snw_ref, o_ref = refs
    else:
        q_ref, k_ref, v_ref, o_ref = refs
    qs = [q_ref[0, 0, s] for s in range(n_sub)]
    tq = qs[0].shape[0]
    dv = v_ref.shape[-1] - LANES

    def step(carry, kb, vb):
        new = []
        for s in range(n_sub):
            m_old, c_old, acc = carry[s]
            sc = _dot_nt(qs[s], kb).astype(BF16)
            m_new = jnp.maximum(m_old, jnp.max(sc, axis=-1, keepdims=True).astype(F32))
            c_new = (m_new - P_SHIFT).astype(BF16)
            p = jnp.exp2(sc - c_new).astype(ATTN_DT)
            alpha = jnp.exp2(c_old - c_new.astype(F32))
            new.append((m_new, c_new.astype(F32), alpha * acc + _dot(p, vb)))
        return tuple(new)

    carry = tuple((jnp.full((tq, 1), -jnp.inf, F32), jnp.full((tq, 1), -jnp.inf, F32),
                   jnp.zeros((tq, dv + LANES), F32)) for _ in range(n_sub))
    for start, rows in blocks:
        carry = step(carry, k_ref[0, 0, start:start + rows, :], v_ref[0, 0, start:start + rows, :])
    outs = [acc[:, :dv] / acc[:, dv:dv + 1] for _, _, acc in carry]
    if n_sub == 1:
        o_ref[0] = outs[0].astype(o_ref.dtype)
    else:
        lv = lam_ref[...]
        lam = (jnp.exp(jnp.sum(lv[0:1] * lv[1:2], axis=-1, keepdims=True))
               - jnp.exp(jnp.sum(lv[2:3] * lv[3:4], axis=-1, keepdims=True)) + lam_init)
        o = outs[0] - lam * outs[1]
        o = _rms(o, snw_ref[...]) * (1.0 - lam_init)
        o_ref[0] = o.astype(o_ref.dtype)


def _flash_call(q, k, v, seq, ctx_queries, extra=(), lam_init=0.0):
    b, nh, n_sub, nt, dk = q.shape
    dve = v.shape[-1]
    dv = dve - LANES
    ctx_len = nt - seq
    if ctx_queries:
        tq, n_rows, q0 = ctx_len, ctx_len, seq // ctx_len
        kv_rows, kv0 = ctx_len, seq // ctx_len
        blocks = ((0, ctx_len),)
    else:
        tq, n_rows, q0 = min(ATTN_TQ, seq), seq, 0
        kv_rows, kv0 = nt, 0
        tkb = min(ATTN_TKB, seq)
        n_blk = seq // tkb
        if n_sub == 2:
            blocks = tuple((j * tkb, tkb + (ctx_len if j == n_blk - 1 else 0)) for j in range(n_blk))
        else:
            blocks = tuple((j * tkb, tkb) for j in range(n_blk)) + ((seq, ctx_len),)
    in_specs = [pl.BlockSpec((1, 1, n_sub, tq, dk), lambda bb, hh, i: (bb, hh, 0, q0 + i, 0)),
                pl.BlockSpec((1, 1, kv_rows, dk), lambda bb, hh, i: (bb, hh, kv0, 0)),
                pl.BlockSpec((1, 1, kv_rows, dve), lambda bb, hh, i: (bb, hh, kv0, 0))]
    in_specs += [_const_spec(e.shape) for e in extra]
    return pl.pallas_call(
        functools.partial(_flash_kernel, n_sub=n_sub, blocks=blocks, lam_init=lam_init),
        grid=(b, nh, n_rows // tq),
        in_specs=in_specs,
        out_specs=pl.BlockSpec((1, tq, dv), lambda bb, hh, i: (bb, i, hh)),
        out_shape=jax.ShapeDtypeStruct((b, n_rows, nh * dv), BF16),
        compiler_params=_cparams("parallel", "parallel", "arbitrary"),
        name="flash_ctx" if ctx_queries else "flash_lat",
    )(q, k, v, *extra)


def _attend(q, k, v, seq, with_ctx, extra=(), lam_init=0.0):
    o = _flash_call(q, k, v, seq, False, extra, lam_init)
    if with_ctx:
        return (o, _flash_call(q, k, v, seq, True, extra, lam_init))
    return o


def _hg_pre_kernel(x_ref, mod_ref, n1_ref, win_ref, lbl_ref, q_ref, i_ref, g_ref, lf_ref, *, layer):
    m = mod_ref[0, 0]
    a = _norm_mod(x_ref[0], n1_ref[...], m[0:1], m[1:2]).astype(BF16)
    width = N_HEADS * LANES
    t = _dot(a, win_ref[...])
    logits = lbl_ref[...]
    e = jnp.exp(logits - jnp.max(logits, axis=0, keepdims=True))
    sm = e / jnp.sum(e, axis=0, keepdims=True)
    lb = jnp.zeros_like(sm[0])
    for j in range(1, layer + 1):
        lb = lb + sm[j]
    q_ref[0] = t[:, :width]
    i_ref[0] = t[:, 3 * width:4 * width]
    g_ref[0] = t[:, 4 * width:]
    for dirn in range(2):
        z = t[:, (1 + dirn) * width:(2 + dirn) * width]
        lbd = lb[dirn:dirn + 1]
        lf_ref[0, dirn] = jnp.log(lbd + (1.0 - lbd) * jax.nn.sigmoid(z))


def _hg_pre_call(h, mods, layer, n_lat_tiles, n1, w_in, lb_logits):
    b, nt, d = h.shape
    tm = TOKEN_TILE
    weights = [n1, w_in, lb_logits]
    tok = jax.ShapeDtypeStruct((b, nt, d), F32)
    return pl.pallas_call(
        functools.partial(_hg_pre_kernel, layer=layer),
        grid=(b, nt // tm),
        in_specs=[_tok_spec(d), _mod_spec(layer, n_lat_tiles, b, d)] + [_const_spec(w.shape) for w in weights],
        out_specs=[_tok_spec(d), _tok_spec(d), _tok_spec(d),
                   pl.BlockSpec((1, 2, tm, d), lambda bb, i: (bb, 0, i, 0))],
        out_shape=[tok, tok, tok, jax.ShapeDtypeStruct((b, 2, nt, d), F32)],
        compiler_params=_cparams("parallel", "parallel"),
        name="hg_pre",
    )(h, mods, *weights)


def _hg_scan_kernel(q_ref, i_ref, lf_ref, o_ref, st_ref, *, rev):
    @pl.when(pl.program_id(1) == 0)
    def _():
        st_ref[...] = jnp.zeros_like(st_ref)

    c = q_ref.shape[1]
    row = lax.broadcasted_iota(jnp.int32, (c, c), 0)
    col = lax.broadcasted_iota(jnp.int32, (c, c), 1)
    tri = jnp.where((col >= row) if rev else (col <= row), 1.0, 0.0).astype(BF16)
    rows = lax.broadcasted_iota(jnp.int32, (c, 1), 0)
    cols = lax.broadcasted_iota(jnp.int32, (1, HG_SUB), 1)
    for bi in range(q_ref.shape[0]):
        q = q_ref[bi]
        v = i_ref[bi]
        lf = lf_ref[bi, 0]
        hi = lf.astype(BF16)
        r1 = lf - hi.astype(F32)
        mid = r1.astype(BF16)
        lo = (r1 - mid.astype(F32)).astype(BF16)
        cum = _dot(tri, hi) + _dot(tri, mid) + _dot(tri, lo)
        key = 1.0 - jnp.exp(lf)
        tot = cum[0:1] if rev else cum[c - 1:c]
        q_dec = (q * jnp.exp(cum)).astype(BF16)
        k_dec = (key * jnp.exp(tot - cum)).astype(BF16)
        v16 = v.astype(BF16)
        for h in range(N_HEADS):
            sl = slice(h * LANES, (h + 1) * LANES)
            st = st_ref[bi, h]
            o = _dot_nt(q_dec[:, sl], st.astype(BF16))
            for jb in range(c // HG_SUB):
                r0 = jb * HG_SUB
                mid_row = r0 + HG_SUB // 2
                ref = cum[mid_row:mid_row + 1, sl]
                kt = (key[r0:r0 + HG_SUB, sl] * jnp.exp(ref - cum[r0:r0 + HG_SUB, sl])).astype(BF16)
                used = (rows < r0 + HG_SUB) if rev else (rows >= r0)
                qt = (q[:, sl] * jnp.exp(jnp.where(used, cum[:, sl] - ref, 0.0))).astype(BF16)
                att = _dot_nt(qt, kt)
                keep = (rows <= r0 + cols) if rev else (rows >= r0 + cols)
                att = jnp.where(keep, att, 0.0).astype(BF16)
                o = o + _dot(att, v16[r0:r0 + HG_SUB, sl])
            o_ref[bi, :, sl] = o
            st_ref[bi, h] = st * jnp.exp(tot[:, sl]) + _dot_tn(v16[:, sl], k_dec[:, sl])


def _hg_scan_call(q, i, lf, seq, rev):
    b, nt, d = q.shape
    c = HG_CHUNK
    n_lat = seq // c
    n_ctx = (nt - seq) // c
    dirn = 1 if rev else 0
    nb = HG_BATCH_GROUP if b % HG_BATCH_GROUP == 0 else 1

    def blk(j):
        if rev:
            return n_lat + n_ctx - 1 - j
        return jnp.where(j < n_ctx, n_lat + j, j - n_ctx)

    return pl.pallas_call(
        functools.partial(_hg_scan_kernel, rev=rev),
        grid=(b // nb, nt // c),
        in_specs=[pl.BlockSpec((nb, c, d), lambda bb, j: (bb, blk(j), 0)),
                  pl.BlockSpec((nb, c, d), lambda bb, j: (bb, blk(j), 0)),
                  pl.BlockSpec((nb, 1, c, d), lambda bb, j: (bb, dirn, blk(j), 0))],
        out_specs=pl.BlockSpec((nb, c, d), lambda bb, j: (bb, blk(j), 0)),
        out_shape=jax.ShapeDtypeStruct((b, nt, d), F32),
        scratch_shapes=[pltpu.VMEM((nb, N_HEADS, LANES, LANES), F32)],
        compiler_params=_cparams("parallel", "arbitrary"),
        name="hg_scan_rev" if rev else "hg_scan_fwd",
    )(q, i, lf)


def _post_kernel(*refs, mode, d_ff, groups, n_lat_tiles):
    tiles, refs = _token_tiles(refs, groups, n_lat_tiles)
    if mode == "hg":
        x, o_f, o_b, g = tiles
        mod_ref, onw_ref, wo_ref, n2_ref, w13_ref, w2_ref, out_ref = refs
        o = (_group_rms(o_f + o_b, onw_ref[...], LANES) * _silu(g)).astype(BF16)
    else:
        x, o = tiles
        mod_ref, wo_ref, n2_ref, w13_ref, w2_ref, out_ref = refs
    m = mod_ref[0, 0]
    x1 = x + m[2:3] * _dot(o, wo_ref[...])
    a2 = _norm_mod(x1, n2_ref[...], m[3:4], m[4:5]).astype(BF16)
    hc = _dot(a2, w13_ref[...])
    hid = (_silu(hc[:, :d_ff]) * hc[:, d_ff:]).astype(BF16)
    out_ref[0] = x1 + m[5:6] * _dot(hid, w2_ref[...])


def _post_call(h, mods, layer, n_lat_tiles, n_out_rows, mixer_out, w_o, n2, w13, w2, onw=None):
    d = mods.shape[-1]
    b = _batch_of(h)
    tm = TOKEN_TILE
    d_ff = w2.shape[0]
    mode = "hg" if onw is not None else "plain"
    specs, flat, groups = _token_inputs([h] + list(mixer_out), n_lat_tiles)
    weights = ([onw] if mode == "hg" else []) + [w_o, n2, w13, w2]
    return pl.pallas_call(
        functools.partial(_post_kernel, mode=mode, d_ff=d_ff, groups=groups, n_lat_tiles=n_lat_tiles),
        grid=(b, n_out_rows // tm),
        in_specs=specs + [_mod_spec(layer, n_lat_tiles, b, d)] + [_const_spec(w.shape) for w in weights],
        out_specs=_tok_spec(d),
        out_shape=jax.ShapeDtypeStruct((b, n_out_rows, d), F32),
        compiler_params=_cparams("parallel", "parallel"),
        name="post_" + mode,
    )(*flat, mods, *weights)


def _rope_tables(seq, ctx_len, reps):
    rows = seq // GRID_W
    row = jnp.repeat(jnp.arange(rows, dtype=F32), GRID_W)
    col = jnp.tile(jnp.arange(GRID_W, dtype=F32), rows)
    axis_dim = ROPE_DIM // 2
    inv_freq = jnp.power(ROPE_BASE, -jnp.arange(0, axis_dim, 2, dtype=F32) / axis_dim)
    ang_r = row[:, None] * inv_freq
    ang_c = col[:, None] * inv_freq
    ang = jnp.concatenate([ang_r, ang_r, ang_c, ang_c], axis=-1)
    cos = jnp.concatenate([jnp.cos(ang), jnp.ones((ctx_len, ROPE_DIM), F32)], axis=0)
    sin = jnp.concatenate([jnp.sin(ang), jnp.zeros((ctx_len, ROPE_DIM), F32)], axis=0)
    return jnp.tile(cos, (1, reps)), jnp.tile(sin, (1, reps))


def _row(w, reps=1):
    return jnp.tile(w.astype(F32), reps)[None, :]


def _pad_groups(w, group, n_groups, to):
    lead = w.shape[:-1]
    w = w.reshape(lead + (n_groups, group))
    w = jnp.pad(w, [(0, 0)] * len(lead) + [(0, 0), (0, to - group)])
    return w.reshape(lead + (n_groups * to,))


def _mla_params(j, mla_w_in, mla_q_norm_w, mla_kv_norm_w, mla_w_uq, mla_w_ukv, mla_qn_w, mla_qr_w,
                mla_kn_w, mla_kr_w):
    hq = MLA_NOPE + MLA_ROPE
    w_uq = mla_w_uq[j].reshape(MLA_Q_RANK, N_HEADS, hq)
    w_ukv = mla_w_ukv[j].reshape(MLA_KV_RANK, N_HEADS, MLA_NOPE + MLA_V)
    return {
        "w_in": jnp.pad(mla_w_in[j], ((0, 0), (0, LANES - MLA_ROPE))).astype(BF16),
        "q_norm_w": _row(mla_q_norm_w[j]),
        "kv_norm_w": _row(mla_kv_norm_w[j]),
        "w_uq_n": w_uq[:, :, :MLA_NOPE].reshape(MLA_Q_RANK, -1).astype(BF16),
        "w_uq_r": _pad_groups(w_uq[:, :, MLA_NOPE:].reshape(MLA_Q_RANK, -1), MLA_ROPE, N_HEADS,
                              LANES).astype(BF16),
        "w_uk": w_ukv[:, :, :MLA_NOPE].reshape(MLA_KV_RANK, -1).astype(BF16),
        "w_uv": w_ukv[:, :, MLA_NOPE:].reshape(MLA_KV_RANK, -1).astype(BF16),
        "qn_w": _row(mla_qn_w[j], N_HEADS),
        "qr_w": _row(_pad_groups(mla_qr_w[j], MLA_ROPE, 1, LANES), N_HEADS),
        "kn_w": _row(mla_kn_w[j], N_HEADS),
        "kr_w": _row(_pad_groups(mla_kr_w[j], MLA_ROPE, 1, LANES)),
    }


def kernel(x, c, ctx, c_ctx, ada_w, ada_b, norm1_w, norm2_w, ffn_w1, ffn_w3, ffn_w2, mla_w_in, mla_q_norm_w, mla_kv_norm_w, mla_w_uq, mla_w_ukv, mla_qn_w, mla_qr_w, mla_kn_w, mla_kr_w, mla_w_o, hg_w_in, hg_lb_logits, hg_o_norm_w, hg_w_o, df_w_qkv, df_qn_w, df_kn_w, df_lambda, df_sub_norm_w, df_w_o):
    b, seq, d = x.shape
    ctx_len = ctx.shape[1]
    depth = ada_w.shape[0]
    tm = TOKEN_TILE
    assert d == N_HEADS * LANES and b < 8
    assert seq % tm == 0 and ctx_len % tm == 0 and seq % GRID_W == 0
    assert seq % ctx_len == 0 and seq % min(ATTN_TQ, seq) == 0 and seq % min(ATTN_TKB, seq) == 0
    assert ctx_len % HG_CHUNK == 0 and seq % HG_CHUNK == 0
    nt = seq + ctx_len
    n_lat_tiles = seq // tm

    h = (x, ctx)
    cvec = jnp.zeros((8, d), F32).at[:b].set(c).at[b].set(c_ctx)
    mods = _ada_call(cvec, ada_w, ada_b).reshape(depth, 8, N_MOD, d)
    cos_t, sin_t = _rope_tables(seq, ctx_len, LANES // ROPE_DIM)

    for i in range(depth):
        kind, j = i % N_MIXERS, i // N_MIXERS
        last = i == depth - 1
        n_out_rows = seq if last else nt
        n1 = _row(norm1_w[i])
        n2 = _row(norm2_w[i])
        w13 = jnp.concatenate([ffn_w1[i], ffn_w3[i]], axis=1).astype(BF16)
        w2 = ffn_w2[i].astype(BF16)
        onw = None
        if kind == 0:
            p = _mla_params(j, mla_w_in, mla_q_norm_w, mla_kv_norm_w, mla_w_uq, mla_w_ukv, mla_qn_w,
                            mla_qr_w, mla_kn_w, mla_kr_w)
            q, k, v = _mla_pre_call(h, mods, i, n_lat_tiles, n1, p, cos_t, sin_t)
            mixer_out = [_attend(q[:, :, None], k, v, seq, not last)]
            w_o = mla_w_o[j].astype(BF16)
        elif kind == 1:
            q, gi, g, lf = _hg_pre_call(h, mods, i, n_lat_tiles, n1, hg_w_in[j].astype(BF16),
                                        hg_lb_logits.astype(F32))
            mixer_out = [_hg_scan_call(q, gi, lf, seq, False), _hg_scan_call(q, gi, lf, seq, True), g]
            onw = _row(hg_o_norm_w[j], N_HEADS)
            w_o = hg_w_o[j].astype(BF16)
        else:
            p = {"w_qkv": df_w_qkv[j].astype(BF16), "qn_w": _row(df_qn_w[j], d // ROPE_DIM),
                 "kn_w": _row(df_kn_w[j], d // ROPE_DIM)}
            q, k, v = _df_pre_call(h, mods, i, n_lat_tiles, n1, p, cos_t, sin_t)
            lam_init = 0.8 - 0.6 * math.exp(-0.3 * i)
            mixer_out = [_attend(q, k, v, seq, not last,
                                 extra=(df_lambda[j].astype(F32), _row(df_sub_norm_w[j])),
                                 lam_init=lam_init)]
            w_o = df_w_o[j].astype(BF16)
        h = _post_call(h, mods, i, n_lat_tiles, n_out_rows, mixer_out, w_o, n2, w13, w2, onw)
    return h
```

```python
import functools
import math

import jax
import jax.numpy as jnp
from jax import lax
from jax.experimental import pallas as pl
from jax.experimental.pallas import tpu as pltpu

F32 = jnp.float32
BF16 = jnp.bfloat16

EPS = 1e-6
ROPE_BASE = 10000.0
GRID_W = 64
N_MOD = 6
N_MIXERS = 3

N_HEADS = 8
LANES = 128
MLA_NOPE = 128
MLA_ROPE = 64
MLA_V = 128
MLA_Q_RANK = 384
MLA_KV_RANK = 256
MLA_DK = 256
ROPE_DIM = 64
ROPE_QUARTER = ROPE_DIM // 4

TOKEN_TILE = 256
ATTN_TQ = 512
ATTN_TKB = 2048
ATTN_DT = jnp.float8_e4m3fn
P_SHIFT = 8.0
LOG2_E = math.log2(math.e)
HG_CHUNK = 128
HG_SUB = 16
HG_BATCH_GROUP = 2
VMEM_LIMIT_BYTES = 56 * 1024 * 1024


def _cparams(*sem):
    return pltpu.CompilerParams(dimension_semantics=sem, vmem_limit_bytes=VMEM_LIMIT_BYTES)


def _const_spec(shape):
    nd = len(shape)
    return pl.BlockSpec(shape, lambda *_: (0,) * nd, pipeline_mode=pl.Buffered(1))


def _dot(a, b):
    return jnp.dot(a, b, preferred_element_type=F32)


def _dot_nt(a, b):
    return lax.dot_general(a, b, (((1,), (1,)), ((), ())), preferred_element_type=F32)


def _dot_tn(a, b):
    return lax.dot_general(a, b, (((0,), (0,)), ((), ())), preferred_element_type=F32)


def _silu(t):
    return t * jax.nn.sigmoid(t)


def _rms(t, w):
    return t * lax.rsqrt(jnp.mean(t * t, axis=-1, keepdims=True) + EPS) * w


def _norm_mod(t, w, shift, scale):
    return _rms(t, w) * (1.0 + scale) + shift


def _group_rms(t, w, group):
    width = t.shape[-1]
    lane = lax.broadcasted_iota(jnp.int32, (1, LANES), 1)
    low = lane < 64
    outs = []
    for p in range(width // LANES):
        ts = t[:, p * LANES:(p + 1) * LANES]
        sq = ts * ts
        if group == LANES:
            ms = jnp.sum(sq, axis=-1, keepdims=True) * (1.0 / LANES)
        else:
            s_lo = jnp.sum(jnp.where(low, sq, 0.0), axis=-1, keepdims=True)
            s_hi = jnp.sum(jnp.where(low, 0.0, sq), axis=-1, keepdims=True)
            ms = jnp.where(low, s_lo, s_hi) * (1.0 / 64)
        outs.append(ts * lax.rsqrt(ms + EPS))
    return jnp.concatenate(outs, axis=-1) * w


def _ones_column(rows):
    lane = lax.broadcasted_iota(jnp.int32, (rows, LANES), 1)
    return jnp.where(lane == 0, 1.0, 0.0).astype(ATTN_DT)


def _rope(t, cos, sin):
    width = t.shape[-1]
    lane = lax.broadcasted_iota(jnp.int32, (1, width), 1)
    up = pltpu.roll(t, width - ROPE_QUARTER, 1)
    dn = pltpu.roll(t, ROPE_QUARTER, 1)
    rot = jnp.where((lane % (2 * ROPE_QUARTER)) < ROPE_QUARTER, -up, dn)
    return t * cos + rot * sin


def _ada_kernel(c_ref, w_ref, b_ref, o_ref):
    s = _silu(c_ref[...]).astype(BF16)
    o_ref[0] = _dot(s, w_ref[0].astype(BF16)) + b_ref[0]


def _ada_call(cvec, ada_w, ada_b):
    depth, d, n = ada_w.shape
    tn = 768
    return pl.pallas_call(
        _ada_kernel,
        grid=(depth, n // tn),
        in_specs=[
            pl.BlockSpec((8, d), lambda l, j: (0, 0)),
            pl.BlockSpec((1, d, tn), lambda l, j: (l, 0, j)),
            pl.BlockSpec((1, 1, tn), lambda l, j: (l, 0, j)),
        ],
        out_specs=pl.BlockSpec((1, 8, tn), lambda l, j: (l, 0, j)),
        out_shape=jax.ShapeDtypeStruct((depth, 8, n), F32),
        compiler_params=_cparams("parallel", "parallel"),
        name="ada_mod",
    )(cvec, ada_w, ada_b.reshape(depth, 1, n))


def _tok_spec(width, tm=TOKEN_TILE):
    return pl.BlockSpec((1, tm, width), lambda b, i: (b, i, 0))


def _mod_spec(layer, n_lat_tiles, ctx_row, d):
    return pl.BlockSpec((1, 1, N_MOD, d),
                        lambda b, i: (layer, jnp.where(i < n_lat_tiles, b, ctx_row), 0, 0))


def _batch_of(a):
    return (a[0] if isinstance(a, tuple) else a).shape[0]


def _token_inputs(arrs, n_lat_tiles, tm=TOKEN_TILE):
    specs, flat, groups = [], [], []
    for a in arrs:
        if isinstance(a, tuple):
            w = a[0].shape[-1]
            specs.append(pl.BlockSpec((1, tm, w), lambda b, i: (b, jnp.minimum(i, n_lat_tiles - 1), 0)))
            specs.append(pl.BlockSpec((1, tm, w), lambda b, i: (b, jnp.maximum(i - n_lat_tiles, 0), 0)))
            flat += list(a)
            groups.append(2)
        else:
            specs.append(_tok_spec(a.shape[-1], tm))
            flat.append(a)
            groups.append(1)
    return specs, flat, tuple(groups)


def _token_tiles(refs, groups, n_lat_tiles):
    is_ctx = pl.program_id(1) >= n_lat_tiles
    tiles, k = [], 0
    for g in groups:
        tiles.append(refs[k][0] if g == 1 else jnp.where(is_ctx, refs[k + 1][0], refs[k][0]))
        k += g
    return tiles, refs[k:]


def _head_spec(width, tm=TOKEN_TILE):
    return pl.BlockSpec((1, N_HEADS, tm, width), lambda b, i: (b, 0, i, 0))


def _mla_pre_kernel(*refs, scale, groups, n_lat_tiles):
    (x,), refs = _token_tiles(refs, groups, n_lat_tiles)
    (mod_ref, n1_ref, win_ref, qnw_ref, kvnw_ref, wuqn_ref, wuqr_ref, wk_ref, wv_ref, qn_ref, qr_ref,
     kn_ref, kr_ref, cos_ref, sin_ref, q_ref, k_ref, v_ref) = refs
    m = mod_ref[0, 0]
    a = _norm_mod(x, n1_ref[...], m[0:1], m[1:2]).astype(BF16)
    t = _dot(a, win_ref[...])
    cq = _rms(t[:, :MLA_Q_RANK], qnw_ref[...]).astype(BF16)
    ckv = _rms(t[:, MLA_Q_RANK:MLA_Q_RANK + MLA_KV_RANK], kvnw_ref[...]).astype(BF16)
    kr = t[:, MLA_Q_RANK + MLA_KV_RANK:]
    cos1 = cos_ref[...]
    sin1 = sin_ref[...]
    cos = jnp.concatenate([cos1] * N_HEADS, axis=-1)
    sin = jnp.concatenate([sin1] * N_HEADS, axis=-1)
    kr = kr * lax.rsqrt(jnp.sum(kr * kr, axis=-1, keepdims=True) * (1.0 / MLA_ROPE) + EPS) * kr_ref[...]
    kr = _rope(kr, cos1, sin1).astype(ATTN_DT)
    qn = _group_rms(_dot(cq, wuqn_ref[...]), qn_ref[...], LANES) * scale
    qr = _group_rms(_dot(cq, wuqr_ref[...]), qr_ref[...], 64)
    qr = _rope(qr, cos, sin) * scale
    kn = _group_rms(_dot(ckv, wk_ref[...]), kn_ref[...], LANES)
    v = _dot(ckv, wv_ref[...])
    for h in range(N_HEADS):
        sl = slice(h * LANES, (h + 1) * LANES)
        q_ref[0, h, :, :LANES] = qn[:, sl].astype(ATTN_DT)
        q_ref[0, h, :, LANES:] = qr[:, sl].astype(ATTN_DT)
        k_ref[0, h, :, :LANES] = kn[:, sl].astype(ATTN_DT)
        k_ref[0, h, :, LANES:] = kr
        v_ref[0, h, :, :LANES] = v[:, sl].astype(ATTN_DT)
        v_ref[0, h, :, LANES:] = _ones_column(v.shape[0])


def _mla_pre_call(h, mods, layer, n_lat_tiles, n1, p, cos_t, sin_t):
    b, d = _batch_of(h), mods.shape[-1]
    nt = cos_t.shape[0]
    tm = TOKEN_TILE
    weights = [n1, p["w_in"], p["q_norm_w"], p["kv_norm_w"], p["w_uq_n"], p["w_uq_r"], p["w_uk"],
               p["w_uv"], p["qn_w"], p["qr_w"], p["kn_w"], p["kr_w"]]
    scale = LOG2_E / math.sqrt(MLA_NOPE + MLA_ROPE)
    specs, flat, groups = _token_inputs([h], n_lat_tiles)
    return pl.pallas_call(
        functools.partial(_mla_pre_kernel, scale=scale, groups=groups, n_lat_tiles=n_lat_tiles),
        grid=(b, nt // tm),
        in_specs=specs + [_mod_spec(layer, n_lat_tiles, b, d)]
        + [_const_spec(w.shape) for w in weights]
        + [pl.BlockSpec((tm, LANES), lambda bb, i: (i, 0))] * 2,
        out_specs=[_head_spec(MLA_DK), _head_spec(MLA_DK), _head_spec(MLA_V + LANES)],
        out_shape=[jax.ShapeDtypeStruct((b, N_HEADS, nt, MLA_DK), ATTN_DT),
                   jax.ShapeDtypeStruct((b, N_HEADS, nt, MLA_DK), ATTN_DT),
                   jax.ShapeDtypeStruct((b, N_HEADS, nt, MLA_V + LANES), ATTN_DT)],
        compiler_params=_cparams("parallel", "parallel"),
        name="mla_pre",
    )(*flat, mods, *weights, cos_t, sin_t)


def _df_pre_kernel(x_ref, mod_ref, n1_ref, wqkv_ref, qn_ref, kn_ref, cos_ref, sin_ref,
                   q_ref, k_ref, v_ref, *, scale):
    m = mod_ref[0, 0]
    a = _norm_mod(x_ref[0], n1_ref[...], m[0:1], m[1:2]).astype(BF16)
    width = N_HEADS * LANES
    t = _dot(a, wqkv_ref[...])
    cos = jnp.concatenate([cos_ref[...]] * N_HEADS, axis=-1)
    sin = jnp.concatenate([sin_ref[...]] * N_HEADS, axis=-1)
    q = _rope(_group_rms(t[:, :width], qn_ref[...], 64), cos, sin) * scale
    k = _rope(_group_rms(t[:, width:2 * width], kn_ref[...], 64), cos, sin)
    v = t[:, 2 * width:]
    lane = lax.broadcasted_iota(jnp.int32, (1, LANES), 1)
    low = lane < 64
    for h in range(N_HEADS):
        sl = slice(h * LANES, (h + 1) * LANES)
        qh = q[:, sl]
        q_ref[0, h, 0] = jnp.where(low, qh, 0.0).astype(ATTN_DT)
        q_ref[0, h, 1] = jnp.where(low, 0.0, qh).astype(ATTN_DT)
        k_ref[0, h] = k[:, sl].astype(ATTN_DT)
        v_ref[0, h, :, :LANES] = v[:, sl].astype(ATTN_DT)
        v_ref[0, h, :, LANES:] = _ones_column(v.shape[0])


def _df_pre_call(h, mods, layer, n_lat_tiles, n1, p, cos_t, sin_t):
    b, nt, d = h.shape
    tm = TOKEN_TILE
    width = N_HEADS * LANES
    weights = [n1, p["w_qkv"], p["qn_w"], p["kn_w"]]
    return pl.pallas_call(
        functools.partial(_df_pre_kernel, scale=LOG2_E / math.sqrt(ROPE_DIM)),
        grid=(b, nt // tm),
        in_specs=[_tok_spec(d), _mod_spec(layer, n_lat_tiles, b, d)]
        + [_const_spec(w.shape) for w in weights]
        + [pl.BlockSpec((tm, LANES), lambda bb, i: (i, 0))] * 2,
        out_specs=[pl.BlockSpec((1, N_HEADS, 2, tm, LANES), lambda bb, i: (bb, 0, 0, i, 0)),
                   _head_spec(LANES), _head_spec(2 * LANES)],
        out_shape=[jax.ShapeDtypeStruct((b, N_HEADS, 2, nt, LANES), ATTN_DT),
                   jax.ShapeDtypeStruct((b, N_HEADS, nt, LANES), ATTN_DT),
                   jax.ShapeDtypeStruct((b, N_HEADS, nt, 2 * LANES), ATTN_DT)],
        compiler_params=_cparams("parallel", "parallel"),
        name="df_pre",
    )(h, mods, *weights, cos_t, sin_t)


def _flash_kernel(*refs, n_sub, blocks, lam_init):
    if n_sub == 2:
        q_ref, k_ref, v_ref, lam_ref, snw_ref, o_ref = refs
    else:
        q_ref, k_ref, v_ref, o_ref = refs
    n_half = 2 if q_ref.shape[3] % 64 == 0 else 1
    tq = q_ref.shape[3] // n_half
    qs = [q_ref[0, 0, s, hf * tq:(hf + 1) * tq, :] for s in range(n_sub) for hf in range(n_half)]
    n_sub_out, n_sub = n_sub, n_sub * n_half
    dv = v_ref.shape[-1] - LANES

    def step(carry, kb, vb):
        new = []
        for s in range(n_sub):
            m_old, c_old, acc = carry[s]
            sc = _dot_nt(qs[s], kb).astype(BF16)
            m_new = jnp.maximum(m_old, jnp.max(sc, axis=-1, keepdims=True).astype(F32))
            c_new = (m_new - P_SHIFT).astype(BF16)
            p = jnp.exp2(sc - c_new).astype(ATTN_DT)
            alpha = jnp.exp2(c_old - c_new.astype(F32))
            new.append((m_new, c_new.astype(F32), alpha * acc + _dot(p, vb)))
        return tuple(new)

    carry = tuple((jnp.full((tq, 1), -jnp.inf, F32), jnp.full((tq, 1), -jnp.inf, F32),
                   jnp.zeros((tq, dv + LANES), F32)) for _ in range(n_sub))
    for start, rows in blocks:
        carry = step(carry, k_ref[0, 0, start:start + rows, :], v_ref[0, 0, start:start + rows, :])
    outs = [acc[:, :dv] / acc[:, dv:dv + 1] for _, _, acc in carry]
    if n_half == 2:
        outs = [jnp.concatenate(outs[2 * s:2 * s + 2], axis=0) for s in range(n_sub_out)]
    n_sub = n_sub_out
    if n_sub == 1:
        o_ref[0] = outs[0].astype(o_ref.dtype)
    else:
        lv = lam_ref[...]
        lam = (jnp.exp(jnp.sum(lv[0:1] * lv[1:2], axis=-1, keepdims=True))
               - jnp.exp(jnp.sum(lv[2:3] * lv[3:4], axis=-1, keepdims=True)) + lam_init)
        o = outs[0] - lam * outs[1]
        o = _rms(o, snw_ref[...]) * (1.0 - lam_init)
        o_ref[0] = o.astype(o_ref.dtype)


def _flash_call(q, k, v, seq, ctx_queries, extra=(), lam_init=0.0):
    b, nh, n_sub, nt, dk = q.shape
    dve = v.shape[-1]
    dv = dve - LANES
    ctx_len = nt - seq
    if ctx_queries:
        tq, n_rows, q0 = ctx_len, ctx_len, seq // ctx_len
        kv_rows, kv0 = ctx_len, seq // ctx_len
        blocks = ((0, ctx_len),)
    else:
        tq, n_rows, q0 = min(ATTN_TQ, seq), seq, 0
        kv_rows, kv0 = nt, 0
        tkb = min(ATTN_TKB, seq)
        n_blk = seq // tkb
        if n_sub == 2:
            blocks = tuple((j * tkb, tkb + (ctx_len if j == n_blk - 1 else 0)) for j in range(n_blk))
        else:
            blocks = tuple((j * tkb, tkb) for j in range(n_blk)) + ((seq, ctx_len),)
    in_specs = [pl.BlockSpec((1, 1, n_sub, tq, dk), lambda bb, hh, i: (bb, hh, 0, q0 + i, 0)),
                pl.BlockSpec((1, 1, kv_rows, dk), lambda bb, hh, i: (bb, hh, kv0, 0)),
                pl.BlockSpec((1, 1, kv_rows, dve), lambda bb, hh, i: (bb, hh, kv0, 0))]
    in_specs += [_const_spec(e.shape) for e in extra]
    return pl.pallas_call(
        functools.partial(_flash_kernel, n_sub=n_sub, blocks=blocks, lam_init=lam_init),
        grid=(b, nh, n_rows // tq),
        in_specs=in_specs,
        out_specs=pl.BlockSpec((1, tq, dv), lambda bb, hh, i: (bb, i, hh)),
        out_shape=jax.ShapeDtypeStruct((b, n_rows, nh * dv), BF16),
        compiler_params=_cparams("parallel", "parallel", "arbitrary"),
        name="flash_ctx" if ctx_queries else "flash_lat",
    )(q, k, v, *extra)


def _attend(q, k, v, seq, with_ctx, extra=(), lam_init=0.0):
    o = _flash_call(q, k, v, seq, False, extra, lam_init)
    if with_ctx:
        return (o, _flash_call(q, k, v, seq, True, extra, lam_init))
    return o


def _hg_pre_kernel(x_ref, mod_ref, n1_ref, win_ref, lbl_ref, q_ref, i_ref, g_ref, lf_ref, *, layer):
    m = mod_ref[0, 0]
    a = _norm_mod(x_ref[0], n1_ref[...], m[0:1], m[1:2]).astype(BF16)
    width = N_HEADS * LANES
    t = _dot(a, win_ref[...])
    logits = lbl_ref[...]
    e = jnp.exp(logits - jnp.max(logits, axis=0, keepdims=True))
    sm = e / jnp.sum(e, axis=0, keepdims=True)
    lb = jnp.zeros_like(sm[0])
    for j in range(1, layer + 1):
        lb = lb + sm[j]
    q_ref[0] = t[:, :width]
    i_ref[0] = t[:, 3 * width:4 * width]
    g_ref[0] = t[:, 4 * width:]
    for dirn in range(2):
        z = t[:, (1 + dirn) * width:(2 + dirn) * width]
        lbd = lb[dirn:dirn + 1]
        lf_ref[0, dirn] = jnp.log(lbd + (1.0 - lbd) * jax.nn.sigmoid(z))


def _hg_pre_call(h, mods, layer, n_lat_tiles, n1, w_in, lb_logits):
    b, nt, d = h.shape
    tm = TOKEN_TILE
    weights = [n1, w_in, lb_logits]
    tok = jax.ShapeDtypeStruct((b, nt, d), F32)
    return pl.pallas_call(
        functools.partial(_hg_pre_kernel, layer=layer),
        grid=(b, nt // tm),
        in_specs=[_tok_spec(d), _mod_spec(layer, n_lat_tiles, b, d)] + [_const_spec(w.shape) for w in weights],
        out_specs=[_tok_spec(d), _tok_spec(d), _tok_spec(d),
                   pl.BlockSpec((1, 2, tm, d), lambda bb, i: (bb, 0, i, 0))],
        out_shape=[tok, tok, tok, jax.ShapeDtypeStruct((b, 2, nt, d), F32)],
        compiler_params=_cparams("parallel", "parallel"),
        name="hg_pre",
    )(h, mods, *weights)


def _hg_scan_kernel(q_ref, i_ref, lf_ref, o_ref, st_ref, *, rev):
    @pl.when(pl.program_id(1) == 0)
    def _():
        st_ref[...] = jnp.zeros_like(st_ref)

    c = q_ref.shape[1]
    row = lax.broadcasted_iota(jnp.int32, (c, c), 0)
    col = lax.broadcasted_iota(jnp.int32, (c, c), 1)
    tri = jnp.where((col >= row) if rev else (col <= row), 1.0, 0.0).astype(BF16)
    rows = lax.broadcasted_iota(jnp.int32, (c, 1), 0)
    cols = lax.broadcasted_iota(jnp.int32, (1, HG_SUB), 1)
    for bi in range(q_ref.shape[0]):
        q = q_ref[bi]
        v = i_ref[bi]
        lf = lf_ref[bi, 0]
        hi = lf.astype(BF16)
        r1 = lf - hi.astype(F32)
        mid = r1.astype(BF16)
        lo = (r1 - mid.astype(F32)).astype(BF16)
        cum = _dot(tri, hi) + _dot(tri, mid) + _dot(tri, lo)
        key = 1.0 - jnp.exp(lf)
        tot = cum[0:1] if rev else cum[c - 1:c]
        q_dec = (q * jnp.exp(cum)).astype(BF16)
        k_dec = (key * jnp.exp(tot - cum)).astype(BF16)
        v16 = v.astype(BF16)
        for h in range(N_HEADS):
            sl = slice(h * LANES, (h + 1) * LANES)
            st = st_ref[bi, h]
            o = _dot_nt(q_dec[:, sl], st.astype(BF16))
            for jb in range(c // HG_SUB):
                r0 = jb * HG_SUB
                mid_row = r0 + HG_SUB // 2
                ref = cum[mid_row:mid_row + 1, sl]
                kt = (key[r0:r0 + HG_SUB, sl] * jnp.exp(ref - cum[r0:r0 + HG_SUB, sl])).astype(BF16)
                used = (rows < r0 + HG_SUB) if rev else (rows >= r0)
                qt = (q[:, sl] * jnp.exp(jnp.where(used, cum[:, sl] - ref, 0.0))).astype(BF16)
                att = _dot_nt(qt, kt)
                keep = (rows <= r0 + cols) if rev else (rows >= r0 + cols)
                att = jnp.where(keep, att, 0.0).astype(BF16)
                o = o + _dot(att, v16[r0:r0 + HG_SUB, sl])
            o_ref[bi, :, sl] = o
            st_ref[bi, h] = st * jnp.exp(tot[:, sl]) + _dot_tn(v16[:, sl], k_dec[:, sl])


def _hg_scan_call(q, i, lf, seq, rev):
    b, nt, d = q.shape
    c = HG_CHUNK
    n_lat = seq // c
    n_ctx = (nt - seq) // c
    dirn = 1 if rev else 0
    nb = HG_BATCH_GROUP if b % HG_BATCH_GROUP == 0 else 1

    def blk(j):
        if rev:
            return n_lat + n_ctx - 1 - j
        return jnp.where(j < n_ctx, n_lat + j, j - n_ctx)

    return pl.pallas_call(
        functools.partial(_hg_scan_kernel, rev=rev),
        grid=(b // nb, nt // c),
        in_specs=[pl.BlockSpec((nb, c, d), lambda bb, j: (bb, blk(j), 0)),
                  pl.BlockSpec((nb, c, d), lambda bb, j: (bb, blk(j), 0)),
                  pl.BlockSpec((nb, 1, c, d), lambda bb, j: (bb, dirn, blk(j), 0))],
        out_specs=pl.BlockSpec((nb, c, d), lambda bb, j: (bb, blk(j), 0)),
        out_shape=jax.ShapeDtypeStruct((b, nt, d), F32),
        scratch_shapes=[pltpu.VMEM((nb, N_HEADS, LANES, LANES), F32)],
        compiler_params=_cparams("parallel", "arbitrary"),
        name="hg_scan_rev" if rev else "hg_scan_fwd",
    )(q, i, lf)


def _post_kernel(*refs, mode, d_ff, groups, n_lat_tiles):
    tiles, refs = _token_tiles(refs, groups, n_lat_tiles)
    if mode == "hg":
        x, o_f, o_b, g = tiles
        mod_ref, onw_ref, wo_ref, n2_ref, w13_ref, w2_ref, out_ref = refs
        o = (_group_rms(o_f + o_b, onw_ref[...], LANES) * _silu(g)).astype(BF16)
    else:
        x, o = tiles
        mod_ref, wo_ref, n2_ref, w13_ref, w2_ref, out_ref = refs
    m = mod_ref[0, 0]
    x1 = x + m[2:3] * _dot(o, wo_ref[...])
    a2 = _norm_mod(x1, n2_ref[...], m[3:4], m[4:5]).astype(BF16)
    hc = _dot(a2, w13_ref[...])
    hid = (_silu(hc[:, :d_ff]) * hc[:, d_ff:]).astype(BF16)
    out_ref[0] = x1 + m[5:6] * _dot(hid, w2_ref[...])


def _post_call(h, mods, layer, n_lat_tiles, n_out_rows, mixer_out, w_o, n2, w13, w2, onw=None):
    d = mods.shape[-1]
    b = _batch_of(h)
    tm = TOKEN_TILE
    d_ff = w2.shape[0]
    mode = "hg" if onw is not None else "plain"
    specs, flat, groups = _token_inputs([h] + list(mixer_out), n_lat_tiles)
    weights = ([onw] if mode == "hg" else []) + [w_o, n2, w13, w2]
    return pl.pallas_call(
        functools.partial(_post_kernel, mode=mode, d_ff=d_ff, groups=groups, n_lat_tiles=n_lat_tiles),
        grid=(b, n_out_rows // tm),
        in_specs=specs + [_mod_spec(layer, n_lat_tiles, b, d)] + [_const_spec(w.shape) for w in weights],
        out_specs=_tok_spec(d),
        out_shape=jax.ShapeDtypeStruct((b, n_out_rows, d), F32),
        compiler_params=_cparams("parallel", "parallel"),
        name="post_" + mode,
    )(*flat, mods, *weights)


def _rope_tables(seq, ctx_len, reps):
    rows = seq // GRID_W
    row = jnp.repeat(jnp.arange(rows, dtype=F32), GRID_W)
    col = jnp.tile(jnp.arange(GRID_W, dtype=F32), rows)
    axis_dim = ROPE_DIM // 2
    inv_freq = jnp.power(ROPE_BASE, -jnp.arange(0, axis_dim, 2, dtype=F32) / axis_dim)
    ang_r = row[:, None] * inv_freq
    ang_c = col[:, None] * inv_freq
    ang = jnp.concatenate([ang_r, ang_r, ang_c, ang_c], axis=-1)
    cos = jnp.concatenate([jnp.cos(ang), jnp.ones((ctx_len, ROPE_DIM), F32)], axis=0)
    sin = jnp.concatenate([jnp.sin(ang), jnp.zeros((ctx_len, ROPE_DIM), F32)], axis=0)
    return jnp.tile(cos, (1, reps)), jnp.tile(sin, (1, reps))


def _row(w, reps=1):
    return jnp.tile(w.astype(F32), reps)[None, :]


def _pad_groups(w, group, n_groups, to):
    lead = w.shape[:-1]
    w = w.reshape(lead + (n_groups, group))
    w = jnp.pad(w, [(0, 0)] * len(lead) + [(0, 0), (0, to - group)])
    return w.reshape(lead + (n_groups * to,))


def _mla_params(j, mla_w_in, mla_q_norm_w, mla_kv_norm_w, mla_w_uq, mla_w_ukv, mla_qn_w, mla_qr_w,
                mla_kn_w, mla_kr_w):
    hq = MLA_NOPE + MLA_ROPE
    w_uq = mla_w_uq[j].reshape(MLA_Q_RANK, N_HEADS, hq)
    w_ukv = mla_w_ukv[j].reshape(MLA_KV_RANK, N_HEADS, MLA_NOPE + MLA_V)
    return {
        "w_in": jnp.pad(mla_w_in[j], ((0, 0), (0, LANES - MLA_ROPE))).astype(BF16),
        "q_norm_w": _row(mla_q_norm_w[j]),
        "kv_norm_w": _row(mla_kv_norm_w[j]),
        "w_uq_n": w_uq[:, :, :MLA_NOPE].reshape(MLA_Q_RANK, -1).astype(BF16),
        "w_uq_r": _pad_groups(w_uq[:, :, MLA_NOPE:].reshape(MLA_Q_RANK, -1), MLA_ROPE, N_HEADS,
                              LANES).astype(BF16),
        "w_uk": w_ukv[:, :, :MLA_NOPE].reshape(MLA_KV_RANK, -1).astype(BF16),
        "w_uv": w_ukv[:, :, MLA_NOPE:].reshape(MLA_KV_RANK, -1).astype(BF16),
        "qn_w": _row(mla_qn_w[j], N_HEADS),
        "qr_w": _row(_pad_groups(mla_qr_w[j], MLA_ROPE, 1, LANES), N_HEADS),
        "kn_w": _row(mla_kn_w[j], N_HEADS),
        "kr_w": _row(_pad_groups(mla_kr_w[j], MLA_ROPE, 1, LANES)),
    }


def kernel(x, c, ctx, c_ctx, ada_w, ada_b, norm1_w, norm2_w, ffn_w1, ffn_w3, ffn_w2, mla_w_in, mla_q_norm_w, mla_kv_norm_w, mla_w_uq, mla_w_ukv, mla_qn_w, mla_qr_w, mla_kn_w, mla_kr_w, mla_w_o, hg_w_in, hg_lb_logits, hg_o_norm_w, hg_w_o, df_w_qkv, df_qn_w, df_kn_w, df_lambda, df_sub_norm_w, df_w_o):
    b, seq, d = x.shape
    ctx_len = ctx.shape[1]
    depth = ada_w.shape[0]
    tm = TOKEN_TILE
    assert d == N_HEADS * LANES and b < 8
    assert seq % tm == 0 and ctx_len % tm == 0 and seq % GRID_W == 0
    assert seq % ctx_len == 0 and seq % min(ATTN_TQ, seq) == 0 and seq % min(ATTN_TKB, seq) == 0
    assert ctx_len % HG_CHUNK == 0 and seq % HG_CHUNK == 0
    nt = seq + ctx_len
    n_lat_tiles = seq // tm

    h = (x, ctx)
    cvec = jnp.zeros((8, d), F32).at[:b].set(c).at[b].set(c_ctx)
    mods = _ada_call(cvec, ada_w, ada_b).reshape(depth, 8, N_MOD, d)
    cos_t, sin_t = _rope_tables(seq, ctx_len, LANES // ROPE_DIM)

    for i in range(depth):
        kind, j = i % N_MIXERS, i // N_MIXERS
        last = i == depth - 1
        n_out_rows = seq if last else nt
        n1 = _row(norm1_w[i])
        n2 = _row(norm2_w[i])
        w13 = jnp.concatenate([ffn_w1[i], ffn_w3[i]], axis=1).astype(BF16)
        w2 = ffn_w2[i].astype(BF16)
        onw = None
        if kind == 0:
            p = _mla_params(j, mla_w_in, mla_q_norm_w, mla_kv_norm_w, mla_w_uq, mla_w_ukv, mla_qn_w,
                            mla_qr_w, mla_kn_w, mla_kr_w)
            q, k, v = _mla_pre_call(h, mods, i, n_lat_tiles, n1, p, cos_t, sin_t)
            mixer_out = [_attend(q[:, :, None], k, v, seq, not last)]
            w_o = mla_w_o[j].astype(BF16)
        elif kind == 1:
            q, gi, g, lf = _hg_pre_call(h, mods, i, n_lat_tiles, n1, hg_w_in[j].astype(BF16),
                                        hg_lb_logits.astype(F32))
            mixer_out = [_hg_scan_call(q, gi, lf, seq, False), _hg_scan_call(q, gi, lf, seq, True), g]
            onw = _row(hg_o_norm_w[j], N_HEADS)
            w_o = hg_w_o[j].astype(BF16)
        else:
            p = {"w_qkv": df_w_qkv[j].astype(BF16), "qn_w": _row(df_qn_w[j], d // ROPE_DIM),
                 "kn_w": _row(df_kn_w[j], d // ROPE_DIM)}
            q, k, v = _df_pre_call(h, mods, i, n_lat_tiles, n1, p, cos_t, sin_t)
            lam_init = 0.8 - 0.6 * math.exp(-0.3 * i)
            mixer_out = [_attend(q, k, v, seq, not last,
                                 extra=(df_lambda[j].astype(F32), _row(df_sub_norm_w[j])),
                                 lam_init=lam_init)]
            w_o = df_w_o[j].astype(BF16)
        h = _post_call(h, mods, i, n_lat_tiles, n_out_rows, mixer_out, w_o, n2, w13, w2, onw)
    return h
```
